```python
import jax, jax.numpy as jnp
from jax import lax
import numpy as np

D_MODEL = 1024
BATCH = 4
SEQ = 4096
DEPTH = 4
DEC_BATCH = 128
DEC_SEQ = 4
PAST_LEN = 2048
PAGE_SIZE = 128

HEAD_DIM = 64
A_HEADS = 4
A_WIDTH = A_HEADS * HEAD_DIM
B_HEADS = 8
B_WIDTH = B_HEADS * HEAD_DIM
C_GROUPS = 4
C_WIDTH = C_GROUPS * HEAD_DIM
MIX_WIDTH = A_WIDTH + B_WIDTH + C_WIDTH
IN_WIDTH = 2 * A_WIDTH + 3 * B_WIDTH + 3 * C_WIDTH
IN_SPLITS = (A_WIDTH, 2 * A_WIDTH, 2 * A_WIDTH + B_WIDTH, 2 * A_WIDTH + 2 * B_WIDTH,
             2 * A_WIDTH + 3 * B_WIDTH, 2 * A_WIDTH + 3 * B_WIDTH + C_WIDTH,
             2 * A_WIDTH + 3 * B_WIDTH + 2 * C_WIDTH)
CHUNK = 128
SB_BLOCK = 128
CONV_K = 3
N_MEM = 256
MEM_HEADS = 4
MEM_HEAD_DIM = D_MODEL // MEM_HEADS
D_FF = 4 * D_MODEL
EPS = 1e-6
SB_BIAS_DIMS = 4
SB_BIAS_SCALE = 3.6

kernel_name = 'hybrid_gmlp_stickbreak_shortconv_decoder_step'


def rmsnorm(x, g):
    xf = x.astype(jnp.float32)
    xf = xf * lax.rsqrt(jnp.mean(xf * xf, axis=-1, keepdims=True) + EPS)
    return xf.astype(x.dtype) * g


def head_layernorm(v, g):
    vf = v.astype(jnp.float32)
    mu = jnp.mean(vf, axis=-1, keepdims=True)
    var = jnp.mean(jnp.square(vf - mu), axis=-1, keepdims=True)
    return ((vf - mu) * lax.rsqrt(var + EPS)).astype(v.dtype) * g


def chunk_spatial_gate(u, v, w_s, b_s):
    nb, t, nh, d = v.shape
    tri = jnp.tril(jnp.ones((CHUNK, CHUNK), dtype=bool))
    w = jnp.where(tri, w_s, 0.0).astype(v.dtype)
    if t >= CHUNK:
        vc = v.reshape(nb, t // CHUNK, CHUNK, nh, d)
        mixed = jnp.einsum('hts,bcshd->bcthd', w, vc) + b_s.T[:, :, None].astype(v.dtype)
        mixed = mixed.reshape(nb, t, nh, d)
    else:
        mixed = jnp.einsum('hts,bshd->bthd', w[:, :t, :t], v) + b_s[:, :t].T[:, :, None].astype(v.dtype)
    return u * mixed


def stick_breaking(q, k, v, q_pos, k_pos):
    z = jnp.einsum('bqhd,bkhd->bhqk', q, k).astype(jnp.float32) * (HEAD_DIM ** -0.5)
    mask = k_pos[None, :] < q_pos[:, None]
    log_beta = jax.nn.log_sigmoid(z)
    log_rest = jnp.where(mask, jax.nn.log_sigmoid(-z), 0.0)
    between = lax.cumsum(log_rest, axis=3, reverse=True) - log_rest
    a = jnp.where(mask, jnp.exp(log_beta + between), 0.0)
    return jnp.einsum('bhqk,bkhd->bqhd', a.astype(v.dtype), v)


def stick_breaking_blocked(q, k, v, q_pos, k_pos):
    nb, t, nh, d = q.shape
    if t <= SB_BLOCK:
        return stick_breaking(q, k, v, q_pos, k_pos)
    n_blk = t // SB_BLOCK
    qb = q.reshape(nb, n_blk, SB_BLOCK, nh, d).transpose(1, 0, 2, 3, 4)
    pb = q_pos.reshape(n_blk, SB_BLOCK)
    out = lax.map(lambda a: stick_breaking(a[0], k, v, a[1], k_pos), (qb, pb))
    return out.transpose(1, 0, 2, 3, 4).reshape(nb, t, nh, d)


def memory_kv(mem, g_mem, w_xkv):
    nb = mem.shape[0]
    kv = rmsnorm(mem, g_mem) @ w_xkv
    mk, mv = jnp.split(kv, 2, axis=-1)
    return (mk.reshape(nb, N_MEM, MEM_HEADS, MEM_HEAD_DIM),
            mv.reshape(nb, N_MEM, MEM_HEADS, MEM_HEAD_DIM))


def trunk_layer(x, mem_k, mem_v, conv_buf, k_past, v_past,
                g_mix, w_in, b_in, g_vnorm, w_spatial, b_spatial, conv_w, w_out,
                g_xattn, w_xq, w_xo, g_ffn, w_up, w_down):
    nb, t, _ = x.shape
    h = rmsnorm(x, g_mix)
    a_u, a_v, b_q, b_k, b_v, c_b, c_c, c_x = jnp.split(h @ w_in + b_in, IN_SPLITS, axis=-1)

    u = jax.nn.gelu(a_u).reshape(nb, t, A_HEADS, HEAD_DIM)
    v_n = head_layernorm(jax.nn.gelu(a_v).reshape(nb, t, A_HEADS, HEAD_DIM), g_vnorm)
    y_a = chunk_spatial_gate(u, v_n, w_spatial, b_spatial).reshape(nb, t, A_WIDTH)

    q = b_q.reshape(nb, t, B_HEADS, HEAD_DIM)
    k_new = b_k.reshape(nb, t, B_HEADS, HEAD_DIM)
    v_new = b_v.reshape(nb, t, B_HEADS, HEAD_DIM)
    if k_past is None:
        pos0 = 0
        k_all, v_all = k_new, v_new
    else:
        pos0 = k_past.shape[1]
        k_all = jnp.concatenate([k_past, k_new], axis=1)
        v_all = jnp.concatenate([v_past, v_new], axis=1)
    q_pos = pos0 + jnp.arange(t, dtype=jnp.int32)
    k_pos = jnp.arange(k_all.shape[1], dtype=jnp.int32)
    y_b = stick_breaking_blocked(q, k_all, v_all, q_pos, k_pos).reshape(nb, t, B_WIDTH)

    gp = jnp.concatenate([conv_buf, c_c * c_x], axis=1)
    conv = sum(conv_w[i] * gp[:, i:i + t] for i in range(CONV_K))
    y_c = c_b * conv
    new_buf = gp[:, t:]

    x = x + jnp.concatenate([y_a, y_b, y_c], axis=-1) @ w_out

    hq = (rmsnorm(x, g_xattn) @ w_xq).reshape(nb, t, MEM_HEADS, MEM_HEAD_DIM)
    s = jnp.einsum('bqhd,bmhd->bhqm', hq, mem_k).astype(jnp.float32) * (MEM_HEAD_DIM ** -0.5)
    pr = jax.nn.softmax(s, axis=-1).astype(mem_v.dtype)
    o = jnp.einsum('bhqm,bmhd->bqhd', pr, mem_v).reshape(nb, t, D_MODEL)
    x = x + o @ w_xo

    f = jnp.square(jax.nn.relu(rmsnorm(x, g_ffn) @ w_up))
    x = x + f @ w_down
    return x, k_new, v_new, new_buf, v_n


def setup_inputs(seed: int = 0) -> dict:
    key = jax.random.key(seed)
    ks = jax.random.split(key, 32)
    n_pages = PAST_LEN // PAGE_SIZE
    n_used = DEC_BATCH * n_pages
    n_pool = n_used + max(1, n_used // 4)

    def nrm(k, shape, s=1.0):
        return jax.random.normal(k, shape, jnp.float32) * s

    def gain(k, shape):
        return 1.0 + 0.02 * jax.random.normal(k, shape, jnp.float32)

    page_table = jax.random.permutation(ks[0], n_pool)[:n_used].reshape(DEC_BATCH, n_pages).astype(jnp.int32)

    head_pat = jnp.zeros((HEAD_DIM,), jnp.float32).at[:SB_BIAS_DIMS].set(SB_BIAS_SCALE)
    q_off = jnp.tile(head_pat, B_HEADS)
    offset = jnp.concatenate([jnp.zeros((2 * A_WIDTH,), jnp.float32), q_off, -q_off,
                              jnp.zeros((B_WIDTH + 3 * C_WIDTH,), jnp.float32)])
    b_in = offset[None, :] + nrm(ks[25], (DEPTH, IN_WIDTH), 0.02)
    k_bias = b_in[:, 2 * A_WIDTH + B_WIDTH:2 * A_WIDTH + 2 * B_WIDTH].reshape(DEPTH, 1, 1, B_HEADS, HEAD_DIM)

    return {
        'x_prompt': nrm(ks[1], (BATCH, SEQ, D_MODEL)),
        'x_sample': nrm(ks[2], (DEC_BATCH, DEC_SEQ, D_MODEL)),
        'cache_sb_k': nrm(ks[3], (DEPTH, n_pool, PAGE_SIZE, B_HEADS, HEAD_DIM)) + k_bias,
        'cache_sb_v': nrm(ks[4], (DEPTH, n_pool, PAGE_SIZE, B_HEADS, HEAD_DIM)),
        'cache_mem_k': nrm(ks[5], (DEPTH, DEC_BATCH, N_MEM, MEM_HEADS, MEM_HEAD_DIM)),
        'cache_mem_v': nrm(ks[6], (DEPTH, DEC_BATCH, N_MEM, MEM_HEADS, MEM_HEAD_DIM)),
        'state_conv': nrm(ks[7], (DEPTH, DEC_BATCH, CONV_K - 1, C_WIDTH)),
        'page_table': page_table,
        'mem_prompt': nrm(ks[8], (BATCH, N_MEM, D_MODEL)),
        'g_mix': gain(ks[9], (DEPTH, D_MODEL)),
        'w_in': nrm(ks[10], (DEPTH, D_MODEL, IN_WIDTH), D_MODEL ** -0.5),
        'b_in': b_in,
        'g_vnorm': gain(ks[11], (DEPTH, A_HEADS, HEAD_DIM)),
        'w_spatial': nrm(ks[12], (DEPTH, A_HEADS, CHUNK, CHUNK), CHUNK ** -0.5),
        'b_spatial': gain(ks[13], (DEPTH, A_HEADS, CHUNK)),
        'conv_w': nrm(ks[14], (DEPTH, CONV_K, C_WIDTH), CONV_K ** -0.5),
        'w_out': nrm(ks[15], (DEPTH, MIX_WIDTH, D_MODEL), MIX_WIDTH ** -0.5),
        'g_xattn': gain(ks[16], (DEPTH, D_MODEL)),
        'g_mem': gain(ks[17], (DEPTH, D_MODEL)),
        'w_xq': nrm(ks[18], (DEPTH, D_MODEL, D_MODEL), D_MODEL ** -0.5),
        'w_xkv': nrm(ks[19], (DEPTH, D_MODEL, 2 * D_MODEL), D_MODEL ** -0.5),
        'w_xo': nrm(ks[20], (DEPTH, D_MODEL, D_MODEL), D_MODEL ** -0.5),
        'g_ffn': gain(ks[21], (DEPTH, D_MODEL)),
        'w_up': nrm(ks[22], (DEPTH, D_MODEL, D_FF), D_MODEL ** -0.5),
        'w_down': nrm(ks[23], (DEPTH, D_FF, D_MODEL), D_FF ** -0.5),
        'g_final': gain(ks[24], (D_MODEL,)),
    }


def reference(x_prompt, x_sample, cache_sb_k, cache_sb_v, cache_mem_k, cache_mem_v, state_conv,
              page_table, mem_prompt, g_mix, w_in, b_in, g_vnorm, w_spatial, b_spatial, conv_w, w_out,
              g_xattn, g_mem, w_xq, w_xkv, w_xo, g_ffn, w_up, w_down, g_final):
    n_seq, n_pages = page_table.shape
    past_len = n_pages * cache_sb_k.shape[2]
    xp, xs = x_prompt, x_sample
    conv_zero = jnp.zeros((x_prompt.shape[0], CONV_K - 1, C_WIDTH), x_prompt.dtype)
    kp_l, vp_l, ks_l, vs_l, cp_l, cs_l, chv_l, mk_l, mv_l = [], [], [], [], [], [], [], [], []
    for l in range(DEPTH):
        lw = (g_mix[l], w_in[l], b_in[l], g_vnorm[l], w_spatial[l], b_spatial[l], conv_w[l], w_out[l],
              g_xattn[l], w_xq[l], w_xo[l], g_ffn[l], w_up[l], w_down[l])
        mk, mv = memory_kv(mem_prompt, g_mem[l], w_xkv[l])
        xp, kp, vp, cbp, _ = trunk_layer(xp, mk, mv, conv_zero, None, None, *lw)
        k_past = cache_sb_k[l][page_table].reshape(n_seq, past_len, B_HEADS, HEAD_DIM)
        v_past = cache_sb_v[l][page_table].reshape(n_seq, past_len, B_HEADS, HEAD_DIM)
        xs, kn, vn, cbs, chv = trunk_layer(xs, cache_mem_k[l], cache_mem_v[l], state_conv[l],
                                           k_past, v_past, *lw)
        kp_l.append(kp); vp_l.append(vp); ks_l.append(kn); vs_l.append(vn)
        cp_l.append(cbp); cs_l.append(cbs); chv_l.append(chv); mk_l.append(mk); mv_l.append(mv)
    y_prompt = rmsnorm(xp, g_final)
    y_sample = rmsnorm(xs, g_final)
    return (y_prompt, y_sample, jnp.stack(kp_l), jnp.stack(vp_l), jnp.stack(ks_l), jnp.stack(vs_l),
            jnp.stack(cp_l), jnp.stack(cs_l), jnp.stack(chv_l), jnp.stack(mk_l), jnp.stack(mv_l))
```

```python
import functools

import jax
import jax.numpy as jnp
from jax import lax
from jax.experimental import pallas as pl
from jax.experimental.pallas import tpu as pltpu

F32 = jnp.float32
BF16 = jnp.bfloat16

EPS = 1e-6
HEAD_DIM = 64
CHUNK = 128
CONV_K = 3
LANES = 128
VMEM_LIMIT = 56 * 1024 * 1024

SB_TQ = 256
SB_TK = 256
SB_PAGES_PER_STEP = 4


def _cparams(sem):
    return pltpu.CompilerParams(dimension_semantics=sem, vmem_limit_bytes=VMEM_LIMIT)


def _rms(x, g):
    ms = jnp.mean(x * x, axis=-1, keepdims=True)
    return (x * lax.rsqrt(ms + EPS)) * g


def _dot(a, b):
    return jnp.dot(a, b, preferred_element_type=F32)


def _dot_nt(a, b):
    return lax.dot_general(a, b, (((1,), (1,)), ((), ())), preferred_element_type=F32)


def _split_dot(x, m, terms):
    out = None
    r = x
    for i in range(terms):
        p = r.astype(BF16)
        d = _dot(p, m)
        out = d if out is None else out + d
        if i + 1 < terms:
            r = r - p.astype(F32)
    return out


def _log_sig_pair(z):
    sp = jnp.log1p(jnp.exp(-jnp.abs(z)))
    lb = jnp.minimum(z, 0.0) - sp
    return lb, lb - z


def _strict_upper(n):
    r = lax.broadcasted_iota(jnp.int32, (n, n), 0)
    c = lax.broadcasted_iota(jnp.int32, (n, n), 1)
    return jnp.where(r > c, 1.0, 0.0).astype(BF16)


def _in_proj_kernel(x_ref, g_ref, wa_ref, ba_ref, wkv_ref, bkv_ref, wc_ref, bc_ref,
                    auv_ref, q_ref, kvt_ref, kvtb_ref, c3_ref, *, aw2, q_scale, tk):
    h = _rms(x_ref[...], g_ref[...]).astype(BF16)
    pa = _dot(h, wa_ref[...]) + ba_ref[...]
    auv_ref[...] = pa[:, :aw2]
    q_ref[...] = (pa[:, aw2:] * q_scale).astype(BF16)
    kvt = _dot_nt(wkv_ref[...], h) + bkv_ref[...]
    kvt_ref[...] = kvt
    for c in range(kvtb_ref.shape[0]):
        kvtb_ref[c] = kvt[:, c * tk:(c + 1) * tk].astype(BF16)
    c3_ref[...] = _dot(h, wc_ref[...]) + bc_ref[...]


def _in_proj(x, g, wa, ba, wkvt, bkvt, wc, bc, *, nb, tm, tk, aw2):
    m, d = x.shape
    t = m // nb
    n_i = t // tm
    na = wa.shape[1]
    bw = na - aw2
    nkv = wkvt.shape[0]
    nc = wc.shape[1]
    const = lambda b, i: (0, 0)
    row = lambda b, i: (b * n_i + i, 0)
    return pl.pallas_call(
        functools.partial(_in_proj_kernel, aw2=aw2, q_scale=HEAD_DIM ** -0.5, tk=tk),
        grid=(nb, n_i),
        in_specs=[
            pl.BlockSpec((tm, d), row),
            pl.BlockSpec((1, d), const),
            pl.BlockSpec((d, na), const), pl.BlockSpec((1, na), const),
            pl.BlockSpec((nkv, d), const), pl.BlockSpec((nkv, 1), const),
            pl.BlockSpec((d, nc), const), pl.BlockSpec((1, nc), const),
        ],
        out_specs=[
            pl.BlockSpec((tm, aw2), row),
            pl.BlockSpec((tm, bw), row),
            pl.BlockSpec((None, nkv, tm), lambda b, i: (b, 0, i)),
            pl.BlockSpec((None, tm // tk, nkv, tk), lambda b, i: (b, i, 0, 0)),
            pl.BlockSpec((tm, nc), row),
        ],
        out_shape=[
            jax.ShapeDtypeStruct((m, aw2), F32),
            jax.ShapeDtypeStruct((m, bw), BF16),
            jax.ShapeDtypeStruct((nb, nkv, t), F32),
            jax.ShapeDtypeStruct((nb, t // tk, nkv, tk), BF16),
            jax.ShapeDtypeStruct((m, nc), F32),
        ],
        compiler_params=_cparams(("parallel", "parallel")),
        name="in_proj",
    )(x, g, wa, ba, wkvt, bkvt, wc, bc)


def _norm_matmul_kernel(x_ref, g_ref, w_ref, *o_refs, scale):
    h = _rms(x_ref[...], g_ref[...]).astype(BF16)
    y = _dot(h, w_ref[...])
    if scale is not None:
        y = y * scale
    for o in o_refs:
        o[...] = y.astype(o.dtype)


def _norm_matmul(x, g, w, *, tm, out_dtypes, scale=None):
    m, d = x.shape
    n = w.shape[1]
    return pl.pallas_call(
        functools.partial(_norm_matmul_kernel, scale=scale),
        grid=(m // tm,),
        in_specs=[pl.BlockSpec((tm, d), lambda i: (i, 0)),
                  pl.BlockSpec((1, d), lambda i: (0, 0)),
                  pl.BlockSpec((d, n), lambda i: (0, 0))],
        out_specs=[pl.BlockSpec((tm, n), lambda i: (i, 0)) for _ in out_dtypes],
        out_shape=[jax.ShapeDtypeStruct((m, n), dt) for dt in out_dtypes],
        compiler_params=_cparams(("parallel",)),
        name="norm_matmul",
    )(x, g, w)


def _matmul_res_kernel(*refs, n_in):
    x_ref = refs[0]
    a_refs = refs[1:1 + n_in]
    w_refs = refs[1 + n_in:1 + 2 * n_in]
    o_ref = refs[1 + 2 * n_in]
    acc = x_ref[...]
    for a, w in zip(a_refs, w_refs):
        acc = acc + _dot(a[...], w[...])
    o_ref[...] = acc


def _matmul_res(x, a_list, w_list, *, tm):
    m, d = x.shape
    n_in = len(a_list)
    in_specs = [pl.BlockSpec((tm, d), lambda i: (i, 0))]
    in_specs += [pl.BlockSpec((tm, a.shape[1]), lambda i: (i, 0)) for a in a_list]
    in_specs += [pl.BlockSpec(w.shape, lambda i: (0, 0)) for w in w_list]
    return pl.pallas_call(
        functools.partial(_matmul_res_kernel, n_in=n_in),
        grid=(m // tm,),
        in_specs=in_specs,
        out_specs=pl.BlockSpec((tm, d), lambda i: (i, 0)),
        out_shape=jax.ShapeDtypeStruct((m, d), F32),
        compiler_params=_cparams(("parallel",)),
        name="matmul_res",
    )(x, *a_list, *w_list)


def _ffn_kernel(x_ref, g_ref, wu_ref, wd_ref, gf_ref, o_ref, h_ref, acc_ref, *, final_norm):
    k = pl.program_id(1)

    @pl.when(k == 0)
    def _():
        x = x_ref[...]
        h_ref[...] = _rms(x, g_ref[...]).astype(BF16)
        acc_ref[...] = x

    u = _dot(h_ref[...], wu_ref[...])
    f = jnp.square(jnp.maximum(u, 0.0)).astype(BF16)
    acc_ref[...] += _dot(f, wd_ref[...])

    @pl.when(k == pl.num_programs(1) - 1)
    def _():
        y = acc_ref[...]
        if final_norm:
            y = _rms(y, gf_ref[...])
        o_ref[...] = y


def _ffn(x, g, wu, wd, gf, *, tm, tf, final_norm):
    m, d = x.shape
    ff = wu.shape[1]
    return pl.pallas_call(
        functools.partial(_ffn_kernel, final_norm=final_norm),
        grid=(m // tm, ff // tf),
        in_specs=[pl.BlockSpec((tm, d), lambda i, k: (i, 0)),
                  pl.BlockSpec((1, d), lambda i, k: (0, 0)),
                  pl.BlockSpec((d, tf), lambda i, k: (0, k)),
                  pl.BlockSpec((tf, d), lambda i, k: (k, 0)),
                  pl.BlockSpec((1, d), lambda i, k: (0, 0))],
        out_specs=pl.BlockSpec((tm, d), lambda i, k: (i, 0)),
        out_shape=jax.ShapeDtypeStruct((m, d), F32),
        scratch_shapes=[pltpu.VMEM((tm, d), BF16), pltpu.VMEM((tm, d), F32)],
        compiler_params=_cparams(("parallel", "arbitrary")),
        name="ffn",
    )(x, g, wu, wd, gf)


def _head_avg_matrix(width):
    r = lax.broadcasted_iota(jnp.int32, (width, width), 0) // HEAD_DIM
    c = lax.broadcasted_iota(jnp.int32, (width, width), 1) // HEAD_DIM
    return jnp.where(r == c, 1.0 / HEAD_DIM, 0.0).astype(BF16)


def _gelu_u_vn(auv, gv, aw):
    u = jax.nn.gelu(auv[:, :aw])
    v = jax.nn.gelu(auv[:, aw:])
    p = _head_avg_matrix(aw)
    mu = _split_dot(v, p, 3)
    dv = v - mu
    var = _split_dot(dv * dv, p, 3)
    return u, (dv * lax.rsqrt(var + EPS)) * gv


def _mixer_prompt_kernel(auv_ref, c3_ref, gv_ref, ws_ref, bs_ref, cw_ref, cbuf_ref,
                         ya_ref, yc_ref, cnew_ref, g_scr, *, aw, cw_width, n_heads):
    i = pl.program_id(1)
    tm = auv_ref.shape[0]

    gv = gv_ref[...]
    lane_head = lax.broadcasted_iota(jnp.int32, (CHUNK, aw), 1) // HEAD_DIM
    tri = (lax.broadcasted_iota(jnp.int32, (CHUNK, CHUNK), 0)
           >= lax.broadcasted_iota(jnp.int32, (CHUNK, CHUNK), 1))
    w_heads = [jnp.where(tri, ws_ref[h], 0.0).astype(BF16) for h in range(n_heads)]
    for c in range(tm // CHUNK):
        rows = pl.ds(c * CHUNK, CHUNK)
        u, vn = _gelu_u_vn(auv_ref[rows, :], gv, aw)
        vb = vn.astype(BF16)
        mixed = bs_ref[...]
        for h in range(n_heads):
            mixed = mixed + _dot(w_heads[h], jnp.where(lane_head == h, vb, jnp.zeros_like(vb)))
        ya_ref[rows, :] = (u * mixed).astype(ya_ref.dtype)

    @pl.when(i == 0)
    def _():
        g_scr[0:8, :] = jnp.zeros((8, cw_width), F32)
        g_scr[6:8, :] = cbuf_ref[...]

    @pl.when(i > 0)
    def _():
        g_scr[0:8, :] = g_scr[tm:tm + 8, :]

    c3 = c3_ref[...]
    gated = c3[:, cw_width:2 * cw_width] * c3[:, 2 * cw_width:]
    g_scr[8:8 + tm, :] = gated
    conv = (cw_ref[0:1, :] * g_scr[6:6 + tm, :] + cw_ref[1:2, :] * g_scr[7:7 + tm, :]
            + cw_ref[2:3, :] * gated)
    yc_ref[...] = (c3[:, :cw_width] * conv).astype(yc_ref.dtype)
    cnew_ref[...] = g_scr[tm + 6:tm + 8, :]


def _mixer_prompt(auv, c3, gv, ws, bs_full, cw, cbuf, *, nb, tm):
    m = auv.shape[0]
    aw = auv.shape[1] // 2
    cww = c3.shape[1] // 3
    n_heads = ws.shape[0]
    n_i = (m // nb) // tm
    row = lambda b, i: (b * n_i + i, 0)
    const2 = lambda b, i: (0, 0)
    return pl.pallas_call(
        functools.partial(_mixer_prompt_kernel, aw=aw, cw_width=cww, n_heads=n_heads),
        grid=(nb, n_i),
        in_specs=[pl.BlockSpec((tm, 2 * aw), row),
                  pl.BlockSpec((tm, 3 * cww), row),
                  pl.BlockSpec((1, aw), const2),
                  pl.BlockSpec(ws.shape, lambda b, i: (0, 0, 0)),
                  pl.BlockSpec(bs_full.shape, const2),
                  pl.BlockSpec(cw.shape, const2),
                  pl.BlockSpec((None, CONV_K - 1, cww), lambda b, i: (b, 0, 0))],
        out_specs=[pl.BlockSpec((tm, aw), row),
                   pl.BlockSpec((tm, cww), row),
                   pl.BlockSpec((None, CONV_K - 1, cww), lambda b, i: (b, 0, 0))],
        out_shape=[jax.ShapeDtypeStruct((m, aw), BF16),
                   jax.ShapeDtypeStruct((m, cww), BF16),
                   jax.ShapeDtypeStruct((nb, CONV_K - 1, cww), F32)],
        scratch_shapes=[pltpu.VMEM((tm + 8, cww), F32)],
        compiler_params=_cparams(("parallel", "arbitrary")),
        name="mixer_prompt",
    )(auv, c3, gv, ws, bs_full, cw, cbuf)


def _mixer_sample_kernel(auv_ref, c3_ref, gv_ref, wv_ref, bv_ref, cw_ref, cbuf_ref,
                         ya_ref, yc_ref, cnew_ref, chv_ref, *, aw, cw_width, n_t, n_seq):
    u, vn = _gelu_u_vn(auv_ref[...], gv_ref[...], aw)
    chv_ref[...] = vn
    c3 = c3_ref[...]
    gated = c3[:, cw_width:2 * cw_width] * c3[:, 2 * cw_width:]
    blk = lambda a, t: a[t * n_seq:(t + 1) * n_seq, :]
    gp = [cbuf_ref[0], cbuf_ref[1]] + [blk(gated, t) for t in range(n_t)]
    for t in range(n_t):
        mixed = bv_ref[t:t + 1, :]
        for s in range(t + 1):
            mixed = mixed + wv_ref[t * n_t + s:t * n_t + s + 1, :] * blk(vn, s)
        rows = pl.ds(t * n_seq, n_seq)
        ya_ref[rows, :] = (blk(u, t) * mixed).astype(ya_ref.dtype)
        conv = cw_ref[0:1, :] * gp[t] + cw_ref[1:2, :] * gp[t + 1] + cw_ref[2:3, :] * gp[t + 2]
        yc_ref[rows, :] = (blk(c3, t)[:, :cw_width] * conv).astype(yc_ref.dtype)
    cnew_ref[0] = gp[n_t]
    cnew_ref[1] = gp[n_t + 1]


def _mixer_sample(auv, c3, gv, wvec, bvec, cw, cbuf_t, *, n_t, n_seq):
    m = auv.shape[0]
    aw = auv.shape[1] // 2
    cww = c3.shape[1] // 3
    return pl.pallas_call(
        functools.partial(_mixer_sample_kernel, aw=aw, cw_width=cww, n_t=n_t, n_seq=n_seq),
        out_shape=[jax.ShapeDtypeStruct((m, aw), BF16),
                   jax.ShapeDtypeStruct((m, cww), BF16),
                   jax.ShapeDtypeStruct((CONV_K - 1, n_seq, cww), F32),
                   jax.ShapeDtypeStruct((m, aw), F32)],
        compiler_params=pltpu.CompilerParams(vmem_limit_bytes=VMEM_LIMIT),
        name="mixer_sample",
    )(auv, c3, gv, wvec, bvec, cw, cbuf_t)


def _sb_block(z, mask, u_mat, carry, terms):
    lb, lr = _log_sig_pair(z)
    if mask is not None:
        lr = jnp.where(mask, lr, 0.0)
    between = _split_dot(lr, u_mat, terms) + carry
    a = jnp.exp(lb + between)
    if mask is not None:
        a = jnp.where(mask, a, 0.0)
    return a, carry + jnp.sum(lr, axis=-1, keepdims=True)


def _sb_prompt_kernel(q_ref, kt_ref, vt_ref, o_ref, carry_ref, acc_ref, *, tq, tk, terms):
    qi = pl.program_id(2)
    u_mat = _strict_upper(tk)
    q = q_ref[...]
    heads = q.shape[1] // HEAD_DIM
    qh = [q[:, h * HEAD_DIM:(h + 1) * HEAD_DIM] for h in range(heads)]
    carry_ref[...] = jnp.zeros_like(carry_ref)
    acc_ref[...] = jnp.zeros_like(acc_ref)

    def step(j, masked):
        for h in range(heads):
            hs = pl.ds(h * HEAD_DIM, HEAD_DIM)
            z = _dot(qh[h], kt_ref[j, hs, :])
            mask = None
            if masked:
                qpos = qi * tq + lax.broadcasted_iota(jnp.int32, (tq, tk), 0)
                kpos = j * tk + lax.broadcasted_iota(jnp.int32, (tq, tk), 1)
                mask = kpos < qpos
            a, carry = _sb_block(z, mask, u_mat, carry_ref[h], terms)
            carry_ref[h] = carry
            acc_ref[h] += _dot_nt(a.astype(BF16), vt_ref[j, hs, :])

    n_diag = tq // tk
    first_diag = qi * n_diag
    for d in range(n_diag - 1, -1, -1):
        step(first_diag + d, True)

    def body(n, c):
        step(first_diag - 1 - n, False)
        return c

    lax.fori_loop(0, first_diag, body, 0)
    o_ref[...] = jnp.concatenate([acc_ref[h] for h in range(heads)], axis=1).astype(o_ref.dtype)


def _sb_prompt(q, kvtb, *, nb, bw, terms):
    m = q.shape[0]
    t = m // nb
    n_kb, tk = kvtb.shape[1], kvtb.shape[3]
    tq = SB_TQ
    n_q = t // tq
    pair = LANES
    n_hp = bw // pair
    return pl.pallas_call(
        functools.partial(_sb_prompt_kernel, tq=tq, tk=tk, terms=terms),
        grid=(nb, n_hp, n_q),
        in_specs=[pl.BlockSpec((tq, pair), lambda b, hp, i: (b * n_q + i, hp)),
                  pl.BlockSpec((None, n_kb, pair, tk), lambda b, hp, i: (b, 0, hp, 0)),
                  pl.BlockSpec((None, n_kb, pair, tk), lambda b, hp, i: (b, 0, n_hp + hp, 0))],
        out_specs=pl.BlockSpec((tq, pair), lambda b, hp, i: (b * n_q + i, hp)),
        out_shape=jax.ShapeDtypeStruct((m, bw), BF16),
        scratch_shapes=[pltpu.VMEM((pair // HEAD_DIM, tq, 1), F32),
                        pltpu.VMEM((pair // HEAD_DIM, tq, HEAD_DIM), F32)],
        compiler_params=_cparams(("parallel", "parallel", "parallel")),
        name="sb_prompt",
    )(q, kvtb, kvtb)


def _sb_sample_kernel(pt_ref, qbd_ref, knt_ref, vnt_ref, *rest, n_pg, n_heads, terms):
    k_refs = rest[:n_pg]
    v_refs = rest[n_pg:2 * n_pg]
    o_ref, carry_ref, acc_ref = rest[2 * n_pg:]
    j = pl.program_id(1)
    qbd = qbd_ref[...]
    rows = qbd.shape[0]
    page = knt_ref.shape[1]
    u_mat = _strict_upper(page)

    def visit(kt, vt, mask):
        z = _dot(qbd, kt.astype(BF16))
        a, carry = _sb_block(z, mask, u_mat, carry_ref[...], terms)
        carry_ref[...] = carry
        acc_ref[...] += _dot_nt(a.astype(BF16), vt.astype(BF16))

    @pl.when(j == 0)
    def _():
        carry_ref[...] = jnp.zeros_like(carry_ref)
        acc_ref[...] = jnp.zeros_like(acc_ref)
        t_idx = lax.broadcasted_iota(jnp.int32, (rows, page), 0) // n_heads
        s_idx = lax.broadcasted_iota(jnp.int32, (rows, page), 1)
        visit(knt_ref[...], vnt_ref[...], s_idx < t_idx)

    for i in range(n_pg):
        visit(k_refs[i][...], v_refs[i][...], None)

    @pl.when(j == pl.num_programs(1) - 1)
    def _():
        acc = acc_ref[...]
        r_head = lax.broadcasted_iota(jnp.int32, acc.shape, 0) % n_heads
        l_head = lax.broadcasted_iota(jnp.int32, acc.shape, 1) // HEAD_DIM
        kept = jnp.where(r_head == l_head, acc, 0.0)
        o_ref[...] = jnp.sum(kept.reshape(rows // n_heads, n_heads, acc.shape[1]), axis=1)


def _sb_sample(page_table_flat, qbd, knt, vnt, cache_kt, cache_vt, layer, *, n_pages, n_heads, terms):
    n_seq, rows, bw = qbd.shape
    page = cache_kt.shape[3]
    n_pg = SB_PAGES_PER_STEP
    n_steps = n_pages // n_pg
    n_t = rows // n_heads

    def page_spec(i):
        def idx(b, j, pt):
            return (layer, pt[b * n_pages + (n_pages - 1 - (j * n_pg + i))], 0, 0)
        return pl.BlockSpec((None, None, bw, page), idx)

    seq3 = lambda b, j, pt: (b, 0, 0)
    grid_spec = pltpu.PrefetchScalarGridSpec(
        num_scalar_prefetch=1,
        grid=(n_seq, n_steps),
        in_specs=[pl.BlockSpec((None, rows, bw), seq3),
                  pl.BlockSpec((None, bw, page), seq3),
                  pl.BlockSpec((None, bw, page), seq3)]
                 + [page_spec(i) for i in range(n_pg)] + [page_spec(i) for i in range(n_pg)],
        out_specs=pl.BlockSpec((None, n_t, bw), seq3),
        scratch_shapes=[pltpu.VMEM((rows, 1), F32), pltpu.VMEM((rows, bw), F32)],
    )
    return pl.pallas_call(
        functools.partial(_sb_sample_kernel, n_pg=n_pg, n_heads=n_heads, terms=terms),
        grid_spec=grid_spec,
        out_shape=jax.ShapeDtypeStruct((n_seq, n_t, bw), F32),
        compiler_params=_cparams(("parallel", "arbitrary")),
        name="sb_sample",
    )(page_table_flat, qbd, knt, vnt, *([cache_kt] * n_pg), *([cache_vt] * n_pg))


def _xattn_prompt_kernel(q_ref, k_ref, v_ref, o_ref, *, n_heads, hd):
    for h in range(n_heads):
        cols = pl.ds(h * hd, hd)
        s = _dot_nt(q_ref[:, cols], k_ref[:, cols])
        e = jnp.exp(s - jnp.max(s, axis=-1, keepdims=True))
        p = e / jnp.sum(e, axis=-1, keepdims=True)
        o_ref[:, cols] = _dot(p.astype(BF16), v_ref[:, cols]).astype(o_ref.dtype)


def _xattn_prompt(q, kv, *, nb, n_mem, n_heads, tm):
    m, d = q.shape
    n_i = (m // nb) // tm
    return pl.pallas_call(
        functools.partial(_xattn_prompt_kernel, n_heads=n_heads, hd=d // n_heads),
        grid=(nb, n_i),
        in_specs=[pl.BlockSpec((tm, d), lambda b, i: (b * n_i + i, 0)),
                  pl.BlockSpec((n_mem, d), lambda b, i: (b, 0)),
                  pl.BlockSpec((n_mem, d), lambda b, i: (b, 1))],
        out_specs=pl.BlockSpec((tm, d), lambda b, i: (b * n_i + i, 0)),
        out_shape=jax.ShapeDtypeStruct((m, d), BF16),
        compiler_params=_cparams(("parallel", "parallel")),
        name="xattn_prompt",
    )(q, kv, kv)


def _xattn_sample_kernel(q_ref, k_ref, v_ref, o_ref, *, n_heads):
    s = _dot_nt(q_ref[...], k_ref[...])
    r_head = lax.broadcasted_iota(jnp.int32, s.shape, 0) % n_heads
    c_head = lax.broadcasted_iota(jnp.int32, s.shape, 1) % n_heads
    own = r_head == c_head
    s = jnp.where(own, s, -jnp.inf)
    e = jnp.where(own, jnp.exp(s - jnp.max(s, axis=-1, keepdims=True)), 0.0)
    p = e / jnp.sum(e, axis=-1, keepdims=True)
    o_ref[...] = _dot(p.astype(BF16), v_ref[...]).astype(o_ref.dtype)


def _xattn_sample(q, k, v, layer, *, n_heads):
    n_seq, rows, hd = q.shape
    mh = k.shape[2]
    return pl.pallas_call(
        functools.partial(_xattn_sample_kernel, n_heads=n_heads),
        grid=(n_seq,),
        in_specs=[pl.BlockSpec((None, rows, hd), lambda b: (b, 0, 0)),
                  pl.BlockSpec((None, None, mh, hd), lambda b: (layer, b, 0, 0)),
                  pl.BlockSpec((None, None, mh, hd), lambda b: (layer, b, 0, 0))],
        out_specs=pl.BlockSpec((None, rows, hd), lambda b: (b, 0, 0)),
        out_shape=jax.ShapeDtypeStruct((n_seq, rows, hd), BF16),
        compiler_params=_cparams(("parallel",)),
        name="xattn_sample",
    )(q, k, v)


def _row_tile(m, cap):
    t = min(m, cap)
    assert m % t == 0
    return t


def kernel(x_prompt, x_sample, cache_sb_k, cache_sb_v, cache_mem_k, cache_mem_v, state_conv, page_table,
           mem_prompt, g_mix, w_in, b_in, g_vnorm, w_spatial, b_spatial, conv_w, w_out, g_xattn, g_mem,
           w_xq, w_xkv, w_xo, g_ffn, w_up, w_down, g_final):
    nb, seq, d = x_prompt.shape
    n_seq, n_t, _ = x_sample.shape
    depth, n_pool, page, b_heads, hd = cache_sb_k.shape
    assert hd == HEAD_DIM and page == LANES
    n_pages = page_table.shape[1]
    a_heads = g_vnorm.shape[1]
    aw = a_heads * HEAD_DIM
    bw = b_heads * HEAD_DIM
    cww = conv_w.shape[2]
    n_mem, mem_heads, mem_hd = cache_mem_k.shape[2:]
    sb_terms = 2

    mp = nb * seq
    ms = n_seq * n_t
    xp = x_prompt.reshape(mp, d)
    xs = x_sample.transpose(1, 0, 2).reshape(ms, d)
    mem = mem_prompt.reshape(nb * n_mem, d)

    cache_kt = cache_sb_k.transpose(0, 1, 3, 4, 2).reshape(depth, n_pool, bw, page)
    cache_vt = cache_sb_v.transpose(0, 1, 3, 4, 2).reshape(depth, n_pool, bw, page)
    cmem_k = cache_mem_k.astype(BF16).reshape(depth, n_seq, n_mem * mem_heads, mem_hd)
    cmem_v = cache_mem_v.astype(BF16).reshape(depth, n_seq, n_mem * mem_heads, mem_hd)
    pt_flat = page_table.reshape(-1)
    head_eye = jnp.eye(b_heads, dtype=BF16)

    row2 = lambda v: v.reshape(1, -1)
    tm_p = _row_tile(seq, 512)
    tm_s = _row_tile(ms, 512)
    ka, kb = 2 * aw + bw, 2 * aw + 3 * bw

    outs = {k: [] for k in ("kp", "vp", "ks", "vs", "cp", "cs", "chv", "mk", "mv")}
    conv_zero = jnp.zeros((nb, CONV_K - 1, cww), F32)

    for l in range(depth):
        last = l == depth - 1
        wa = w_in[l][:, :ka].astype(BF16)
        wkvt = w_in[l][:, ka:kb].T.astype(BF16)
        wc = w_in[l][:, kb:].astype(BF16)
        ba, bkvt, bc = row2(b_in[l][:ka]), b_in[l][ka:kb].reshape(-1, 1), row2(b_in[l][kb:])
        wo = w_out[l].astype(BF16)
        wo_parts = [wo[:aw], wo[aw:aw + bw], wo[aw + bw:]]
        wxq, wxo = w_xq[l].astype(BF16), w_xo[l].astype(BF16)
        wu, wd = w_up[l].astype(BF16), w_down[l].astype(BF16)
        gv = g_vnorm[l].reshape(1, aw)
        bs_full = jnp.repeat(b_spatial[l].T, HEAD_DIM, axis=1)
        wvec = jnp.repeat(w_spatial[l][:, :n_t, :n_t].transpose(1, 2, 0), HEAD_DIM, axis=2).reshape(n_t * n_t, aw)
        bvec = jnp.repeat(b_spatial[l][:, :n_t].T, HEAD_DIM, axis=1)
        xq_scale = mem_hd ** -0.5

        kv_f32, kv_b = _norm_matmul(mem, row2(g_mem[l]), w_xkv[l].astype(BF16),
                                    tm=_row_tile(nb * n_mem, 512), out_dtypes=(F32, BF16))
        auv, q, kvt, kvtb, c3 = _in_proj(xp, row2(g_mix[l]), wa, ba, wkvt, bkvt, wc, bc,
                                         nb=nb, tm=tm_p, tk=SB_TK, aw2=2 * aw)
        ya, yc, cnew = _mixer_prompt(auv, c3, gv, w_spatial[l], bs_full, conv_w[l], conv_zero, nb=nb, tm=tm_p)
        yb = _sb_prompt(q, kvtb, nb=nb, bw=bw, terms=sb_terms)
        xp = _matmul_res(xp, [ya, yb, yc], wo_parts, tm=tm_p)
        hq, = _norm_matmul(xp, row2(g_xattn[l]), wxq, tm=tm_p, out_dtypes=(BF16,), scale=xq_scale)
        o = _xattn_prompt(hq, kv_b, nb=nb, n_mem=n_mem, n_heads=mem_heads, tm=tm_p)
        xp = _matmul_res(xp, [o], [wxo], tm=tm_p)
        xp = _ffn(xp, row2(g_ffn[l]), wu, wd, row2(g_final), tm=_row_tile(mp, 1024), tf=1024, final_norm=last)
        outs["kp"].append(kvt[:, :bw].reshape(nb, b_heads, HEAD_DIM, seq).transpose(0, 3, 1, 2))
        outs["vp"].append(kvt[:, bw:].reshape(nb, b_heads, HEAD_DIM, seq).transpose(0, 3, 1, 2))
        outs["cp"].append(cnew)
        outs["mk"].append(kv_f32[:, :d].reshape(nb, n_mem, mem_heads, mem_hd))
        outs["mv"].append(kv_f32[:, d:].reshape(nb, n_mem, mem_heads, mem_hd))

        auv, q, kvt, _, c3 = _in_proj(xs, row2(g_mix[l]), wa, ba, wkvt, bkvt, wc, bc,
                                      nb=1, tm=tm_s, tk=min(SB_TK, tm_s), aw2=2 * aw)
        ya, yc, cnew, chv = _mixer_sample(auv, c3, gv, wvec, bvec, conv_w[l],
                                          state_conv[l].transpose(1, 0, 2), n_t=n_t, n_seq=n_seq)
        q_b = q.reshape(n_t, n_seq, b_heads, HEAD_DIM).transpose(1, 0, 2, 3)
        qbd = (q_b[:, :, :, None, :] * head_eye[None, None, :, :, None]).reshape(n_seq, n_t * b_heads, bw)
        kvt_b = kvt[0].reshape(2 * bw, n_t, n_seq).transpose(2, 0, 1)
        kvt_pad = jnp.pad(kvt_b, ((0, 0), (0, 0), (0, page - n_t)))
        yb_b = _sb_sample(pt_flat, qbd, kvt_pad[:, :bw], kvt_pad[:, bw:], cache_kt, cache_vt, l,
                          n_pages=n_pages, n_heads=b_heads, terms=sb_terms)
        yb = yb_b.transpose(1, 0, 2).reshape(ms, bw).astype(BF16)
        xs = _matmul_res(xs, [ya, yb, yc], wo_parts, tm=tm_s)
        hq, = _norm_matmul(xs, row2(g_xattn[l]), wxq, tm=tm_s, out_dtypes=(BF16,), scale=xq_scale)
        hq_b = hq.reshape(n_t, n_seq, mem_heads, mem_hd).transpose(1, 0, 2, 3).reshape(n_seq, n_t * mem_heads, mem_hd)
        o_b = _xattn_sample(hq_b, cmem_k, cmem_v, l, n_heads=mem_heads)
        o = o_b.reshape(n_seq, n_t, d).transpose(1, 0, 2).reshape(ms, d)
        xs = _matmul_res(xs, [o], [wxo], tm=tm_s)
        xs = _ffn(xs, row2(g_ffn[l]), wu, wd, row2(g_final), tm=tm_s, tf=1024, final_norm=last)
        kv_new = kvt_b.reshape(n_seq, 2, b_heads, HEAD_DIM, n_t).transpose(1, 0, 4, 2, 3)
        outs["ks"].append(kv_new[0])
        outs["vs"].append(kv_new[1])
        outs["cs"].append(cnew.transpose(1, 0, 2))
        outs["chv"].append(chv.reshape(n_t, n_seq, a_heads, HEAD_DIM).transpose(1, 0, 2, 3))

    y_prompt = xp.reshape(nb, seq, d)
    y_sample = xs.reshape(n_t, n_seq, d).transpose(1, 0, 2)
    st = lambda k: jnp.stack(outs[k])
    return (y_prompt, y_sample, st("kp"), st("vp"), st("ks"), st("vs"), st("cp"), st("cs"), st("chv"),
            st("mk"), st("mv"))
```

```python
import functools

import jax
import jax.numpy as jnp
from jax import lax
from jax.experimental import pallas as pl
from jax.experimental.pallas import tpu as pltpu

F32 = jnp.float32
BF16 = jnp.bfloat16

EPS = 1e-6
LOG2E = 1.4426950408889634
HEAD_DIM = 64
CHUNK = 128
CONV_K = 3
LANES = 128
VMEM_LIMIT = 56 * 1024 * 1024

SB_TQ = 512
SB_TK = 256
SB_HEADS_PER_STEP = 8


def _cparams(sem):
    return pltpu.CompilerParams(dimension_semantics=sem, vmem_limit_bytes=VMEM_LIMIT)


def _rms(x, g):
    ms = jnp.mean(x * x, axis=-1, keepdims=True)
    return (x * lax.rsqrt(ms + EPS)) * g


def _dot(a, b):
    return jnp.dot(a, b, preferred_element_type=F32)


def _dot_nt(a, b):
    return lax.dot_general(a, b, (((1,), (1,)), ((), ())), preferred_element_type=F32)


def _split_dot(x, m, terms):
    out = None
    r = x
    for i in range(terms):
        p = r.astype(BF16)
        d = _dot(p, m)
        out = d if out is None else out + d
        if i + 1 < terms:
            r = r - p.astype(F32)
    return out


def _neg_abs(x):
    bits = lax.bitcast_convert_type(x, jnp.uint32) | jnp.uint32(0x80000000)
    return lax.bitcast_convert_type(bits, F32)


def _log2_sig_pair(z2):
    sp = jnp.log(1.0 + jnp.exp2(_neg_abs(z2))) * LOG2E
    lb = jnp.minimum(z2, 0.0) - sp
    return lb, lb - z2


def _strict_upper(n):
    r = lax.broadcasted_iota(jnp.int32, (n, n), 0)
    c = lax.broadcasted_iota(jnp.int32, (n, n), 1)
    return jnp.where(r > c, 1.0, 0.0).astype(BF16)


def _in_proj_kernel(x_ref, g_ref, wa_ref, ba_ref, wkv_ref, bkv_ref, wc_ref, bc_ref,
                    auv_ref, q_ref, c3_ref, *kv_refs, aw2, bw, q_scale, tk, feature_major):
    h = _rms(x_ref[...], g_ref[...]).astype(BF16)
    pa = _dot(h, wa_ref[...]) + ba_ref[...]
    auv_ref[...] = pa[:, :aw2]
    q_ref[...] = (pa[:, aw2:] * q_scale).astype(BF16)
    c3_ref[...] = _dot(h, wc_ref[...]) + bc_ref[...]
    if feature_major:
        kt_ref, vt_ref, kvtb_ref = kv_refs
        kvt = _dot_nt(wkv_ref[...], h) + bkv_ref[...]
        kt_ref[...] = kvt[:bw]
        vt_ref[...] = kvt[bw:]
        for c in range(kvtb_ref.shape[0]):
            kvtb_ref[c] = kvt[:, c * tk:(c + 1) * tk].astype(BF16)
    else:
        kv_refs[0][...] = _dot(h, wkv_ref[...]) + bkv_ref[...]


def _in_proj(x, g, wa, ba, wkv, bkv, wc, bc, *, nb, tm, tk, aw2, feature_major):
    m, d = x.shape
    t = m // nb
    n_i = t // tm
    na = wa.shape[1]
    bw = na - aw2
    nc = wc.shape[1]
    const = lambda b, i: (0, 0)
    row = lambda b, i: (b * n_i + i, 0)
    out_specs = [pl.BlockSpec((tm, aw2), row), pl.BlockSpec((tm, bw), row), pl.BlockSpec((tm, nc), row)]
    out_shape = [jax.ShapeDtypeStruct((m, aw2), F32), jax.ShapeDtypeStruct((m, bw), BF16),
                 jax.ShapeDtypeStruct((m, nc), F32)]
    if feature_major:
        out_specs += [pl.BlockSpec((None, bw, tm), lambda b, i: (b, 0, i)),
                      pl.BlockSpec((None, bw, tm), lambda b, i: (b, 0, i)),
                      pl.BlockSpec((None, tm // tk, 2 * bw, tk), lambda b, i: (b, i, 0, 0))]
        out_shape += [jax.ShapeDtypeStruct((nb, bw, t), F32), jax.ShapeDtypeStruct((nb, bw, t), F32),
                      jax.ShapeDtypeStruct((nb, t // tk, 2 * bw, tk), BF16)]
    else:
        out_specs += [pl.BlockSpec((tm, 2 * bw), row)]
        out_shape += [jax.ShapeDtypeStruct((m, 2 * bw), F32)]
    return pl.pallas_call(
        functools.partial(_in_proj_kernel, aw2=aw2, bw=bw, q_scale=HEAD_DIM ** -0.5 * LOG2E, tk=tk,
                          feature_major=feature_major),
        grid=(nb, n_i),
        in_specs=[
            pl.BlockSpec((tm, d), row),
            pl.BlockSpec((1, d), const),
            pl.BlockSpec((d, na), const), pl.BlockSpec((1, na), const),
            pl.BlockSpec(wkv.shape, const), pl.BlockSpec(bkv.shape, const),
            pl.BlockSpec((d, nc), const), pl.BlockSpec((1, nc), const),
        ],
        out_specs=out_specs,
        out_shape=out_shape,
        compiler_params=_cparams(("parallel", "parallel")),
        name="in_proj",
    )(x, g, wa, ba, wkv, bkv, wc, bc)


def _norm_matmul_kernel(x_ref, g_ref, w_ref, *o_refs, scale):
    h = _rms(x_ref[...], g_ref[...]).astype(BF16)
    y = _dot(h, w_ref[...])
    if scale is not None:
        y = y * scale
    for o in o_refs:
        o[...] = y.astype(o.dtype)


def _norm_matmul(x, g, w, *, tm, out_dtypes, scale=None):
    m, d = x.shape
    n = w.shape[1]
    return pl.pallas_call(
        functools.partial(_norm_matmul_kernel, scale=scale),
        grid=(m // tm,),
        in_specs=[pl.BlockSpec((tm, d), lambda i: (i, 0)),
                  pl.BlockSpec((1, d), lambda i: (0, 0)),
                  pl.BlockSpec((d, n), lambda i: (0, 0))],
        out_specs=[pl.BlockSpec((tm, n), lambda i: (i, 0)) for _ in out_dtypes],
        out_shape=[jax.ShapeDtypeStruct((m, n), dt) for dt in out_dtypes],
        compiler_params=_cparams(("parallel",)),
        name="norm_matmul",
    )(x, g, w)


def _matmul_res_kernel(*refs, n_in):
    x_ref = refs[0]
    a_refs = refs[1:1 + n_in]
    w_refs = refs[1 + n_in:1 + 2 * n_in]
    o_ref = refs[1 + 2 * n_in]
    acc = x_ref[...]
    for a, w in zip(a_refs, w_refs):
        acc = acc + _dot(a[...], w[...])
    o_ref[...] = acc


def _matmul_res(x, a_list, w_list, *, tm):
    m, d = x.shape
    n_in = len(a_list)
    in_specs = [pl.BlockSpec((tm, d), lambda i: (i, 0))]
    in_specs += [pl.BlockSpec((tm, a.shape[1]), lambda i: (i, 0)) for a in a_list]
    in_specs += [pl.BlockSpec(w.shape, lambda i: (0, 0)) for w in w_list]
    return pl.pallas_call(
        functools.partial(_matmul_res_kernel, n_in=n_in),
        grid=(m // tm,),
        in_specs=in_specs,
        out_specs=pl.BlockSpec((tm, d), lambda i: (i, 0)),
        out_shape=jax.ShapeDtypeStruct((m, d), F32),
        compiler_params=_cparams(("parallel",)),
        name="matmul_res",
    )(x, *a_list, *w_list)


def _ffn_kernel(x_ref, g_ref, wu_ref, wd_ref, gf_ref, o_ref, h_ref, acc_ref, *, final_norm):
    k = pl.program_id(1)

    @pl.when(k == 0)
    def _():
        x = x_ref[...]
        h_ref[...] = _rms(x, g_ref[...]).astype(BF16)
        acc_ref[...] = x

    u = _dot(h_ref[...], wu_ref[...])
    f = jnp.square(jnp.maximum(u, 0.0)).astype(BF16)
    acc_ref[...] += _dot(f, wd_ref[...])

    @pl.when(k == pl.num_programs(1) - 1)
    def _():
        y = acc_ref[...]
        if final_norm:
            y = _rms(y, gf_ref[...])
        o_ref[...] = y


def _ffn(x, g, wu, wd, gf, *, tm, tf, final_norm):
    m, d = x.shape
    ff = wu.shape[1]
    return pl.pallas_call(
        functools.partial(_ffn_kernel, final_norm=final_norm),
        grid=(m // tm, ff // tf),
        in_specs=[pl.BlockSpec((tm, d), lambda i, k: (i, 0)),
                  pl.BlockSpec((1, d), lambda i, k: (0, 0)),
                  pl.BlockSpec((d, tf), lambda i, k: (0, k)),
                  pl.BlockSpec((tf, d), lambda i, k: (k, 0)),
                  pl.BlockSpec((1, d), lambda i, k: (0, 0))],
        out_specs=pl.BlockSpec((tm, d), lambda i, k: (i, 0)),
        out_shape=jax.ShapeDtypeStruct((m, d), F32),
        scratch_shapes=[pltpu.VMEM((tm, d), BF16), pltpu.VMEM((tm, d), F32)],
        compiler_params=_cparams(("parallel", "arbitrary")),
        name="ffn",
    )(x, g, wu, wd, gf)


def _head_avg_matrix(width):
    r = lax.broadcasted_iota(jnp.int32, (width, width), 0) // HEAD_DIM
    c = lax.broadcasted_iota(jnp.int32, (width, width), 1) // HEAD_DIM
    return jnp.where(r == c, 1.0 / HEAD_DIM, 0.0).astype(BF16)


def _gelu_u_vn(auv, gv, aw):
    u = jax.nn.gelu(auv[:, :aw])
    v = jax.nn.gelu(auv[:, aw:])
    p = _head_avg_matrix(aw)
    mu = _split_dot(v, p, 3)
    dv = v - mu
    var = _split_dot(dv * dv, p, 3)
    return u, (dv * lax.rsqrt(var + EPS)) * gv


def _mixer_prompt_kernel(auv_ref, c3_ref, gv_ref, ws_ref, bs_ref, cw_ref, cbuf_ref,
                         ya_ref, yc_ref, cnew_ref, g_scr, *, aw, cw_width, n_heads):
    i = pl.program_id(1)
    tm = auv_ref.shape[0]

    gv = gv_ref[...]
    lane_head = lax.broadcasted_iota(jnp.int32, (CHUNK, aw), 1) // HEAD_DIM
    tri = (lax.broadcasted_iota(jnp.int32, (CHUNK, CHUNK), 0)
           >= lax.broadcasted_iota(jnp.int32, (CHUNK, CHUNK), 1))
    w_heads = [jnp.where(tri, ws_ref[h], 0.0).astype(BF16) for h in range(n_heads)]
    for c in range(tm // CHUNK):
        rows = pl.ds(c * CHUNK, CHUNK)
        u, vn = _gelu_u_vn(auv_ref[rows, :], gv, aw)
        vb = vn.astype(BF16)
        mixed = bs_ref[...]
        for h in range(n_heads):
            mixed = mixed + _dot(w_heads[h], jnp.where(lane_head == h, vb, jnp.zeros_like(vb)))
        ya_ref[rows, :] = (u * mixed).astype(ya_ref.dtype)

    @pl.when(i == 0)
    def _():
        g_scr[0:8, :] = jnp.zeros((8, cw_width), F32)
        g_scr[6:8, :] = cbuf_ref[...]

    @pl.when(i > 0)
    def _():
        g_scr[0:8, :] = g_scr[tm:tm + 8, :]

    c3 = c3_ref[...]
    gated = c3[:, cw_width:2 * cw_width] * c3[:, 2 * cw_width:]
    g_scr[8:8 + tm, :] = gated
    conv = (cw_ref[0:1, :] * g_scr[6:6 + tm, :] + cw_ref[1:2, :] * g_scr[7:7 + tm, :]
            + cw_ref[2:3, :] * gated)
    yc_ref[...] = (c3[:, :cw_width] * conv).astype(yc_ref.dtype)
    cnew_ref[...] = g_scr[tm + 6:tm + 8, :]


def _mixer_prompt(auv, c3, gv, ws, bs_full, cw, cbuf, *, nb, tm):
    m = auv.shape[0]
    aw = auv.shape[1] // 2
    cww = c3.shape[1] // 3
    n_heads = ws.shape[0]
    n_i = (m // nb) // tm
    row = lambda b, i: (b * n_i + i, 0)
    const2 = lambda b, i: (0, 0)
    return pl.pallas_call(
        functools.partial(_mixer_prompt_kernel, aw=aw, cw_width=cww, n_heads=n_heads),
        grid=(nb, n_i),
        in_specs=[pl.BlockSpec((tm, 2 * aw), row),
                  pl.BlockSpec((tm, 3 * cww), row),
                  pl.BlockSpec((1, aw), const2),
                  pl.BlockSpec(ws.shape, lambda b, i: (0, 0, 0)),
                  pl.BlockSpec(bs_full.shape, const2),
                  pl.BlockSpec(cw.shape, const2),
                  pl.BlockSpec((None, CONV_K - 1, cww), lambda b, i: (b, 0, 0))],
        out_specs=[pl.BlockSpec((tm, aw), row),
                   pl.BlockSpec((tm, cww), row),
                   pl.BlockSpec((None, CONV_K - 1, cww), lambda b, i: (b, 0, 0))],
        out_shape=[jax.ShapeDtypeStruct((m, aw), BF16),
                   jax.ShapeDtypeStruct((m, cww), BF16),
                   jax.ShapeDtypeStruct((nb, CONV_K - 1, cww), F32)],
        scratch_shapes=[pltpu.VMEM((tm + 8, cww), F32)],
        compiler_params=_cparams(("parallel", "arbitrary")),
        name="mixer_prompt",
    )(auv, c3, gv, ws, bs_full, cw, cbuf)


def _mixer_sample_kernel(auv_ref, c3_ref, gv_ref, wv_ref, bv_ref, cw_ref, cbuf_ref,
                         ya_ref, yc_ref, cnew_ref, chv_ref, *, aw, cw_width, n_t, n_seq):
    u, vn = _gelu_u_vn(auv_ref[...], gv_ref[...], aw)
    chv_ref[...] = vn
    c3 = c3_ref[...]
    gated = c3[:, cw_width:2 * cw_width] * c3[:, 2 * cw_width:]
    blk = lambda a, t: a[t * n_seq:(t + 1) * n_seq, :]
    gp = [cbuf_ref[0], cbuf_ref[1]] + [blk(gated, t) for t in range(n_t)]
    for t in range(n_t):
        mixed = bv_ref[t:t + 1, :]
        for s in range(t + 1):
            mixed = mixed + wv_ref[t * n_t + s:t * n_t + s + 1, :] * blk(vn, s)
        rows = pl.ds(t * n_seq, n_seq)
        ya_ref[rows, :] = (blk(u, t) * mixed).astype(ya_ref.dtype)
        conv = cw_ref[0:1, :] * gp[t] + cw_ref[1:2, :] * gp[t + 1] + cw_ref[2:3, :] * gp[t + 2]
        yc_ref[rows, :] = (blk(c3, t)[:, :cw_width] * conv).astype(yc_ref.dtype)
    cnew_ref[0] = gp[n_t]
    cnew_ref[1] = gp[n_t + 1]


def _mixer_sample(auv, c3, gv, wvec, bvec, cw, cbuf_t, *, n_t, n_seq):
    m = auv.shape[0]
    aw = auv.shape[1] // 2
    cww = c3.shape[1] // 3
    return pl.pallas_call(
        functools.partial(_mixer_sample_kernel, aw=aw, cw_width=cww, n_t=n_t, n_seq=n_seq),
        out_shape=[jax.ShapeDtypeStruct((m, aw), BF16),
                   jax.ShapeDtypeStruct((m, cww), BF16),
                   jax.ShapeDtypeStruct((CONV_K - 1, n_seq, cww), F32),
                   jax.ShapeDtypeStruct((m, aw), F32)],
        compiler_params=pltpu.CompilerParams(vmem_limit_bytes=VMEM_LIMIT),
        name="mixer_sample",
    )(auv, c3, gv, wvec, bvec, cw, cbuf_t)


def _sb_block(z2, mask, u_mat, carry, terms):
    lb, lr = _log2_sig_pair(z2)
    if mask is not None:
        lr = jnp.where(mask, lr, 0.0)
    between = _split_dot(lr, u_mat, terms) + carry
    a = jnp.exp2(lb + between)
    if mask is not None:
        a = jnp.where(mask, a, 0.0)
    return a, carry + jnp.sum(lr, axis=-1, keepdims=True)


def _sb_prompt_kernel(q_ref, kt_ref, vt_ref, o_ref, carry_ref, acc_ref, *, tq, tk, terms):
    qi = pl.program_id(2)
    u_mat = _strict_upper(tk)
    q = q_ref[...]
    heads = q.shape[1] // HEAD_DIM
    qh = [q[:, h * HEAD_DIM:(h + 1) * HEAD_DIM] for h in range(heads)]
    carry_ref[...] = jnp.zeros_like(carry_ref)
    acc_ref[...] = jnp.zeros_like(acc_ref)

    def step(j, r0, nr, triangular):
        rows = pl.ds(r0, nr)
        mask = None
        if triangular:
            mask = (lax.broadcasted_iota(jnp.int32, (nr, tk), 1)
                    < lax.broadcasted_iota(jnp.int32, (nr, tk), 0))
        for h in range(heads):
            hs = pl.ds(h * HEAD_DIM, HEAD_DIM)
            z = _dot(qh[h][r0:r0 + nr], kt_ref[j, hs, :])
            a, carry = _sb_block(z, mask, u_mat, carry_ref[h, rows], terms)
            carry_ref[h, rows] = carry
            acc_ref[h, rows] += _dot_nt(a.astype(BF16), vt_ref[j, hs, :])

    n_diag = tq // tk
    first_diag = qi * n_diag
    for r in range(n_diag):
        step(first_diag + r, r * tk, tk, True)
        for d in range(r - 1, -1, -1):
            step(first_diag + d, r * tk, tk, False)

    def body(n, c):
        step(first_diag - 1 - n, 0, tq, False)
        return c

    lax.fori_loop(0, first_diag, body, 0)
    o_ref[...] = jnp.concatenate([acc_ref[h] for h in range(heads)], axis=1).astype(o_ref.dtype)


def _sb_prompt(q, kvtb, *, nb, bw, terms):
    m = q.shape[0]
    t = m // nb
    n_kb, tk = kvtb.shape[1], kvtb.shape[3]
    tq = SB_TQ
    n_q = t // tq
    gw = SB_HEADS_PER_STEP * HEAD_DIM
    n_hg = bw // gw
    return pl.pallas_call(
        functools.partial(_sb_prompt_kernel, tq=tq, tk=tk, terms=terms),
        grid=(nb, n_hg, n_q),
        in_specs=[pl.BlockSpec((tq, gw), lambda b, g, i: (b * n_q + i, g)),
                  pl.BlockSpec((None, n_kb, gw, tk), lambda b, g, i: (b, 0, g, 0)),
                  pl.BlockSpec((None, n_kb, gw, tk), lambda b, g, i: (b, 0, n_hg + g, 0))],
        out_specs=pl.BlockSpec((tq, gw), lambda b, g, i: (b * n_q + i, g)),
        out_shape=jax.ShapeDtypeStruct((m, bw), BF16),
        scratch_shapes=[pltpu.VMEM((SB_HEADS_PER_STEP, tq, 1), F32),
                        pltpu.VMEM((SB_HEADS_PER_STEP, tq, HEAD_DIM), F32)],
        compiler_params=_cparams(("parallel", "parallel", "parallel")),
        name="sb_prompt",
    )(q, kvtb, kvtb)


def _sb_sample_kernel(pt_ref, qbd_ref, kvn_ref, *rest, n_pages, n_heads, bw):
    k_refs = rest[:n_pages]
    v_refs = rest[n_pages:2 * n_pages]
    o_ref = rest[2 * n_pages]
    qbd = qbd_ref[...]
    rows = qbd.shape[0]
    page = k_refs[0].shape[1]
    n_slots = n_pages + 1
    u_mat = _strict_upper(page)

    kvn = kvn_ref[...]
    pad = jnp.zeros((page - kvn.shape[0], bw), F32)
    kn = jnp.concatenate([kvn[:, :bw], pad], axis=0).astype(BF16)
    vn = jnp.concatenate([kvn[:, bw:], pad], axis=0).astype(BF16)
    t_idx = lax.broadcasted_iota(jnp.int32, (rows, page), 0) // n_heads
    new_mask = lax.broadcasted_iota(jnp.int32, (rows, page), 1) < t_idx

    z = [_dot_nt(qbd, kn)] + [_dot(qbd, k_refs[i][...].astype(BF16)) for i in range(n_pages)]
    lb, lr = _log2_sig_pair(jnp.concatenate(z, axis=1))
    lr = [lr[:, s * page:(s + 1) * page] for s in range(n_slots)]
    lr[0] = jnp.where(new_mask, lr[0], 0.0)
    between = _dot(jnp.concatenate(lr, axis=0).astype(BF16), u_mat)
    carry = jnp.zeros((rows, 1), F32)
    x = []
    for s in range(n_slots):
        x.append(lb[:, s * page:(s + 1) * page] + between[s * rows:(s + 1) * rows] + carry)
        carry = carry + jnp.sum(lr[s], axis=-1, keepdims=True)
    a = jnp.exp2(jnp.concatenate(x, axis=1))
    a = [a[:, s * page:(s + 1) * page] for s in range(n_slots)]
    acc = _dot(jnp.where(new_mask, a[0], 0.0).astype(BF16), vn)
    for i in range(n_pages):
        acc = acc + _dot_nt(a[i + 1].astype(BF16), v_refs[i][...].astype(BF16))

    r_head = lax.broadcasted_iota(jnp.int32, acc.shape, 0) % n_heads
    l_head = lax.broadcasted_iota(jnp.int32, acc.shape, 1) // HEAD_DIM
    kept = jnp.where(r_head == l_head, acc, 0.0)
    o_ref[...] = jnp.sum(kept.reshape(rows // n_heads, n_heads, bw), axis=1)


def _sb_sample(page_table_flat, qbd, kv_new, cache_kt, cache_vt, layer, *, n_pages, n_heads):
    n_seq, rows, bw = qbd.shape
    page = cache_kt.shape[3]
    n_t = rows // n_heads

    def page_spec(i):
        def idx(b, pt):
            return (layer, pt[b * n_pages + (n_pages - 1 - i)], 0, 0)
        return pl.BlockSpec((None, None, bw, page), idx)

    seq3 = lambda b, pt: (b, 0, 0)
    grid_spec = pltpu.PrefetchScalarGridSpec(
        num_scalar_prefetch=1,
        grid=(n_seq,),
        in_specs=[pl.BlockSpec((None, rows, bw), seq3),
                  pl.BlockSpec((None,) + kv_new.shape[1:], seq3)]
                 + [page_spec(i) for i in range(n_pages)] + [page_spec(i) for i in range(n_pages)],
        out_specs=pl.BlockSpec((None, n_t, bw), seq3),
    )
    return pl.pallas_call(
        functools.partial(_sb_sample_kernel, n_pages=n_pages, n_heads=n_heads, bw=bw),
        grid_spec=grid_spec,
        out_shape=jax.ShapeDtypeStruct((n_seq, n_t, bw), F32),
        compiler_params=_cparams(("parallel",)),
        name="sb_sample",
    )(page_table_flat, qbd, kv_new, *([cache_kt] * n_pages), *([cache_vt] * n_pages))


def _xattn_prompt_kernel(q_ref, k_ref, v_ref, o_ref, *, n_heads, hd):
    for h in range(n_heads):
        cols = pl.ds(h * hd, hd)
        s = _dot_nt(q_ref[:, cols], k_ref[:, cols])
        e = jnp.exp(s - jnp.max(s, axis=-1, keepdims=True))
        p = e / jnp.sum(e, axis=-1, keepdims=True)
        o_ref[:, cols] = _dot(p.astype(BF16), v_ref[:, cols]).astype(o_ref.dtype)


def _xattn_prompt(q, kv, *, nb, n_mem, n_heads, tm):
    m, d = q.shape
    n_i = (m // nb) // tm
    return pl.pallas_call(
        functools.partial(_xattn_prompt_kernel, n_heads=n_heads, hd=d // n_heads),
        grid=(nb, n_i),
        in_specs=[pl.BlockSpec((tm, d), lambda b, i: (b * n_i + i, 0)),
                  pl.BlockSpec((n_mem, d), lambda b, i: (b, 0)),
                  pl.BlockSpec((n_mem, d), lambda b, i: (b, 1))],
        out_specs=pl.BlockSpec((tm, d), lambda b, i: (b * n_i + i, 0)),
        out_shape=jax.ShapeDtypeStruct((m, d), BF16),
        compiler_params=_cparams(("parallel", "parallel")),
        name="xattn_prompt",
    )(q, kv, kv)


def _xattn_sample_kernel(q_ref, k_ref, v_ref, o_ref, *, n_heads):
    n_mem, _, hd = k_ref.shape
    k = k_ref[...].reshape(n_mem * n_heads, hd).astype(BF16)
    v = v_ref[...].reshape(n_mem * n_heads, hd).astype(BF16)
    s = _dot_nt(q_ref[...], k)
    r_head = lax.broadcasted_iota(jnp.int32, s.shape, 0) % n_heads
    c_head = lax.broadcasted_iota(jnp.int32, s.shape, 1) % n_heads
    s = jnp.where(r_head == c_head, s, -jnp.inf)
    e = jnp.exp(s - jnp.max(s, axis=-1, keepdims=True))
    p = e / jnp.sum(e, axis=-1, keepdims=True)
    o_ref[...] = _dot(p.astype(BF16), v).astype(o_ref.dtype)


def _xattn_sample(q, k, v, layer, *, n_heads):
    n_seq, rows, hd = q.shape
    n_mem = k.shape[2]
    kv_spec = pl.BlockSpec((None, None, n_mem, n_heads, hd), lambda b: (layer, b, 0, 0, 0))
    return pl.pallas_call(
        functools.partial(_xattn_sample_kernel, n_heads=n_heads),
        grid=(n_seq,),
        in_specs=[pl.BlockSpec((None, rows, hd), lambda b: (b, 0, 0)), kv_spec, kv_spec],
        out_specs=pl.BlockSpec((None, rows, hd), lambda b: (b, 0, 0)),
        out_shape=jax.ShapeDtypeStruct((n_seq, rows, hd), BF16),
        compiler_params=_cparams(("parallel",)),
        name="xattn_sample",
    )(q, k, v)


def _row_tile(m, cap):
    t = min(m, cap)
    assert m % t == 0
    return t


def kernel(x_prompt, x_sample, cache_sb_k, cache_sb_v, cache_mem_k, cache_mem_v, state_conv, page_table,
           mem_prompt, g_mix, w_in, b_in, g_vnorm, w_spatial, b_spatial, conv_w, w_out, g_xattn, g_mem,
           w_xq, w_xkv, w_xo, g_ffn, w_up, w_down, g_final):
    nb, seq, d = x_prompt.shape
    n_seq, n_t, _ = x_sample.shape
    depth, n_pool, page, b_heads, hd = cache_sb_k.shape
    assert hd == HEAD_DIM and page == LANES
    n_pages = page_table.shape[1]
    a_heads = g_vnorm.shape[1]
    aw = a_heads * HEAD_DIM
    bw = b_heads * HEAD_DIM
    cww = conv_w.shape[2]
    n_mem, mem_heads, mem_hd = cache_mem_k.shape[2:]
    sb_terms = 1

    mp = nb * seq
    ms = n_seq * n_t
    xp = x_prompt.reshape(mp, d)
    xs = x_sample.transpose(1, 0, 2).reshape(ms, d)
    mem = mem_prompt.reshape(nb * n_mem, d)

    cache_kt = cache_sb_k.transpose(0, 1, 3, 4, 2).reshape(depth, n_pool, bw, page)
    cache_vt = cache_sb_v.transpose(0, 1, 3, 4, 2).reshape(depth, n_pool, bw, page)
    pt_flat = page_table.reshape(-1)
    head_eye = jnp.eye(b_heads, dtype=BF16)

    row2 = lambda v: v.reshape(1, -1)
    tm_p = _row_tile(seq, 512)
    tm_s = _row_tile(ms, 512)
    ka, kb = 2 * aw + bw, 2 * aw + 3 * bw

    outs = {k: [] for k in ("kp", "vp", "ks", "vs", "cp", "cs", "chv", "mk", "mv")}
    conv_zero = jnp.zeros((nb, CONV_K - 1, cww), F32)

    for l in range(depth):
        last = l == depth - 1
        wa = w_in[l][:, :ka].astype(BF16)
        wkv = w_in[l][:, ka:kb].astype(BF16)
        wkvt = wkv.T
        wc = w_in[l][:, kb:].astype(BF16)
        ba, bkv, bc = row2(b_in[l][:ka]), row2(b_in[l][ka:kb]), row2(b_in[l][kb:])
        bkvt = bkv.reshape(-1, 1)
        wo = w_out[l].astype(BF16)
        wo_parts = [wo[:aw], wo[aw:aw + bw], wo[aw + bw:]]
        wxq, wxo = w_xq[l].astype(BF16), w_xo[l].astype(BF16)
        wu, wd = w_up[l].astype(BF16), w_down[l].astype(BF16)
        gv = g_vnorm[l].reshape(1, aw)
        bs_full = jnp.repeat(b_spatial[l].T, HEAD_DIM, axis=1)
        wvec = jnp.repeat(w_spatial[l][:, :n_t, :n_t].transpose(1, 2, 0), HEAD_DIM, axis=2).reshape(n_t * n_t, aw)
        bvec = jnp.repeat(b_spatial[l][:, :n_t].T, HEAD_DIM, axis=1)
        xq_scale = mem_hd ** -0.5

        kv_f32, kv_b = _norm_matmul(mem, row2(g_mem[l]), w_xkv[l].astype(BF16),
                                    tm=_row_tile(nb * n_mem, 512), out_dtypes=(F32, BF16))
        auv, q, c3, kt, vt, kvtb = _in_proj(xp, row2(g_mix[l]), wa, ba, wkvt, bkvt, wc, bc,
                                            nb=nb, tm=tm_p, tk=SB_TK, aw2=2 * aw, feature_major=True)
        ya, yc, cnew = _mixer_prompt(auv, c3, gv, w_spatial[l], bs_full, conv_w[l], conv_zero, nb=nb, tm=tm_p)
        yb = _sb_prompt(q, kvtb, nb=nb, bw=bw, terms=sb_terms)
        xp = _matmul_res(xp, [ya, yb, yc], wo_parts, tm=tm_p)
        hq, = _norm_matmul(xp, row2(g_xattn[l]), wxq, tm=tm_p, out_dtypes=(BF16,), scale=xq_scale)
        o = _xattn_prompt(hq, kv_b, nb=nb, n_mem=n_mem, n_heads=mem_heads, tm=tm_p)
        xp = _matmul_res(xp, [o], [wxo], tm=tm_p)
        xp = _ffn(xp, row2(g_ffn[l]), wu, wd, row2(g_final), tm=_row_tile(mp, 1024), tf=1024, final_norm=last)
        outs["kp"].append(kt.reshape(nb, b_heads, HEAD_DIM, seq).transpose(0, 3, 1, 2))
        outs["vp"].append(vt.reshape(nb, b_heads, HEAD_DIM, seq).transpose(0, 3, 1, 2))
        outs["cp"].append(cnew)
        outs["mk"].append(kv_f32[:, :d].reshape(nb, n_mem, mem_heads, mem_hd))
        outs["mv"].append(kv_f32[:, d:].reshape(nb, n_mem, mem_heads, mem_hd))

        auv, q, c3, kv = _in_proj(xs, row2(g_mix[l]), wa, ba, wkv, bkv, wc, bc,
                                  nb=1, tm=tm_s, tk=SB_TK, aw2=2 * aw, feature_major=False)
        ya, yc, cnew, chv = _mixer_sample(auv, c3, gv, wvec, bvec, conv_w[l],
                                          state_conv[l].transpose(1, 0, 2), n_t=n_t, n_seq=n_seq)
        q_b = q.reshape(n_t, n_seq, b_heads, HEAD_DIM).transpose(1, 0, 2, 3)
        qbd = (q_b[:, :, :, None, :] * head_eye[None, None, :, :, None]).reshape(n_seq, n_t * b_heads, bw)
        kv_b = kv.reshape(n_t, n_seq, 2 * bw).transpose(1, 0, 2)
        kv_pad = jnp.pad(kv_b, ((0, 0), (0, 8 - n_t), (0, 0)))
        yb_b = _sb_sample(pt_flat, qbd, kv_pad, cache_kt, cache_vt, l,
                          n_pages=n_pages, n_heads=b_heads)
        yb = yb_b.transpose(1, 0, 2).reshape(ms, bw).astype(BF16)
        xs = _matmul_res(xs, [ya, yb, yc], wo_parts, tm=tm_s)
        hq, = _norm_matmul(xs, row2(g_xattn[l]), wxq, tm=tm_s, out_dtypes=(BF16,), scale=xq_scale)
        hq_b = hq.reshape(n_t, n_seq, mem_heads, mem_hd).transpose(1, 0, 2, 3).reshape(n_seq, n_t * mem_heads, mem_hd)
        o_b = _xattn_sample(hq_b, cache_mem_k, cache_mem_v, l, n_heads=mem_heads)
        o = o_b.reshape(n_seq, n_t, d).transpose(1, 0, 2).reshape(ms, d)
        xs = _matmul_res(xs, [o], [wxo], tm=tm_s)
        xs = _ffn(xs, row2(g_ffn[l]), wu, wd, row2(g_final), tm=tm_s, tf=1024, final_norm=last)
        outs["ks"].append(kv_b[:, :, :bw].reshape(n_seq, n_t, b_heads, HEAD_DIM))
        outs["vs"].append(kv_b[:, :, bw:].reshape(n_seq, n_t, b_heads, HEAD_DIM))
        outs["cs"].append(cnew.transpose(1, 0, 2))
        outs["chv"].append(chv.reshape(n_t, n_seq, a_heads, HEAD_DIM).transpose(1, 0, 2, 3))

    y_prompt = xp.reshape(nb, seq, d)
    y_sample = xs.reshape(n_t, n_seq, d).transpose(1, 0, 2)
    st = lambda k: jnp.stack(outs[k])
    return (y_prompt, y_sample, st("kp"), st("vp"), st("ks"), st("vs"), st("cp"), st("cs"), st("chv"),
            st("mk"), st("mv"))
```

```python
import functools

import jax
import jax.numpy as jnp
from jax import lax
from jax.experimental import pallas as pl
from jax.experimental.pallas import tpu as pltpu

F32 = jnp.float32
BF16 = jnp.bfloat16

EPS = 1e-6
LOG2E = 1.4426950408889634
HEAD_DIM = 64
CHUNK = 128
CONV_K = 3
LANES = 128
VMEM_LIMIT = 56 * 1024 * 1024

SB_TQ = 512
SB_TK = 256
SB_HEADS_PER_STEP = 8
XATTN_SEQS_PER_STEP = 2


def _cparams(sem):
    return pltpu.CompilerParams(dimension_semantics=sem, vmem_limit_bytes=VMEM_LIMIT)


def _rms(x, g):
    ms = jnp.mean(x * x, axis=-1, keepdims=True)
    return (x * lax.rsqrt(ms + EPS)) * g


def _dot(a, b):
    return jnp.dot(a, b, preferred_element_type=F32)


def _dot_nt(a, b):
    return lax.dot_general(a, b, (((1,), (1,)), ((), ())), preferred_element_type=F32)


def _split_dot(x, m, terms):
    out = None
    r = x
    for i in range(terms):
        p = r.astype(BF16)
        d = _dot(p, m)
        out = d if out is None else out + d
        if i + 1 < terms:
            r = r - p.astype(F32)
    return out


def _log2_sig_pair(z2):
    m = jnp.minimum(z2, 0.0)
    sp = jnp.log(1.0 + jnp.exp2((m + m) - z2)) * LOG2E
    lb = m - sp
    return lb, lb - z2


def _strict_upper(n):
    r = lax.broadcasted_iota(jnp.int32, (n, n), 0)
    c = lax.broadcasted_iota(jnp.int32, (n, n), 1)
    return jnp.where(r > c, 1.0, 0.0).astype(BF16)


def _in_proj_kernel(x_ref, g_ref, wa_ref, ba_ref, wkv_ref, bkv_ref, wc_ref, bc_ref,
                    auv_ref, q_ref, c3_ref, *kv_refs, aw2, bw, q_scale, tk, feature_major):
    h = _rms(x_ref[...], g_ref[...]).astype(BF16)
    pa = _dot(h, wa_ref[...]) + ba_ref[...]
    auv_ref[...] = pa[:, :aw2]
    q_ref[...] = (pa[:, aw2:] * q_scale).astype(BF16)
    c3_ref[...] = _dot(h, wc_ref[...]) + bc_ref[...]
    if feature_major:
        kt_ref, vt_ref, kvtb_ref = kv_refs
        kvt = _dot_nt(wkv_ref[...], h) + bkv_ref[...]
        kt_ref[...] = kvt[:bw]
        vt_ref[...] = kvt[bw:]
        for c in range(kvtb_ref.shape[0]):
            kvtb_ref[c] = kvt[:, c * tk:(c + 1) * tk].astype(BF16)
    else:
        kv_refs[0][...] = _dot(h, wkv_ref[...]) + bkv_ref[...]


def _in_proj(x, g, wa, ba, wkv, bkv, wc, bc, *, nb, tm, tk, aw2, feature_major):
    m, d = x.shape
    t = m // nb
    n_i = t // tm
    na = wa.shape[1]
    bw = na - aw2
    nc = wc.shape[1]
    const = lambda b, i: (0, 0)
    row = lambda b, i: (b * n_i + i, 0)
    out_specs = [pl.BlockSpec((tm, aw2), row), pl.BlockSpec((tm, bw), row), pl.BlockSpec((tm, nc), row)]
    out_shape = [jax.ShapeDtypeStruct((m, aw2), F32), jax.ShapeDtypeStruct((m, bw), BF16),
                 jax.ShapeDtypeStruct((m, nc), F32)]
    if feature_major:
        out_specs += [pl.BlockSpec((None, bw, tm), lambda b, i: (b, 0, i)),
                      pl.BlockSpec((None, bw, tm), lambda b, i: (b, 0, i)),
                      pl.BlockSpec((None, tm // tk, 2 * bw, tk), lambda b, i: (b, i, 0, 0))]
        out_shape += [jax.ShapeDtypeStruct((nb, bw, t), F32), jax.ShapeDtypeStruct((nb, bw, t), F32),
                      jax.ShapeDtypeStruct((nb, t // tk, 2 * bw, tk), BF16)]
    else:
        out_specs += [pl.BlockSpec((tm, 2 * bw), row)]
        out_shape += [jax.ShapeDtypeStruct((m, 2 * bw), F32)]
    return pl.pallas_call(
        functools.partial(_in_proj_kernel, aw2=aw2, bw=bw, q_scale=HEAD_DIM ** -0.5 * LOG2E, tk=tk,
                          feature_major=feature_major),
        grid=(nb, n_i),
        in_specs=[
            pl.BlockSpec((tm, d), row),
            pl.BlockSpec((1, d), const),
            pl.BlockSpec((d, na), const), pl.BlockSpec((1, na), const),
            pl.BlockSpec(wkv.shape, const), pl.BlockSpec(bkv.shape, const),
            pl.BlockSpec((d, nc), const), pl.BlockSpec((1, nc), const),
        ],
        out_specs=out_specs,
        out_shape=out_shape,
        compiler_params=_cparams(("parallel", "parallel")),
        name="in_proj",
    )(x, g, wa, ba, wkv, bkv, wc, bc)


def _norm_matmul_kernel(x_ref, g_ref, w_ref, *o_refs, scale):
    h = _rms(x_ref[...], g_ref[...]).astype(BF16)
    y = _dot(h, w_ref[...])
    if scale is not None:
        y = y * scale
    for o in o_refs:
        o[...] = y.astype(o.dtype)


def _norm_matmul(x, g, w, *, tm, out_dtypes, scale=None):
    m, d = x.shape
    n = w.shape[1]
    return pl.pallas_call(
        functools.partial(_norm_matmul_kernel, scale=scale),
        grid=(m // tm,),
        in_specs=[pl.BlockSpec((tm, d), lambda i: (i, 0)),
                  pl.BlockSpec((1, d), lambda i: (0, 0)),
                  pl.BlockSpec((d, n), lambda i: (0, 0))],
        out_specs=[pl.BlockSpec((tm, n), lambda i: (i, 0)) for _ in out_dtypes],
        out_shape=[jax.ShapeDtypeStruct((m, n), dt) for dt in out_dtypes],
        compiler_params=_cparams(("parallel",)),
        name="norm_matmul",
    )(x, g, w)


def _matmul_res_kernel(*refs, n_in):
    x_ref = refs[0]
    a_refs = refs[1:1 + n_in]
    w_refs = refs[1 + n_in:1 + 2 * n_in]
    o_ref = refs[1 + 2 * n_in]
    acc = x_ref[...]
    for a, w in zip(a_refs, w_refs):
        acc = acc + _dot(a[...], w[...])
    o_ref[...] = acc


def _matmul_res(x, a_list, w_list, *, tm):
    m, d = x.shape
    n_in = len(a_list)
    in_specs = [pl.BlockSpec((tm, d), lambda i: (i, 0))]
    in_specs += [pl.BlockSpec((tm, a.shape[1]), lambda i: (i, 0)) for a in a_list]
    in_specs += [pl.BlockSpec(w.shape, lambda i: (0, 0)) for w in w_list]
    return pl.pallas_call(
        functools.partial(_matmul_res_kernel, n_in=n_in),
        grid=(m // tm,),
        in_specs=in_specs,
        out_specs=pl.BlockSpec((tm, d), lambda i: (i, 0)),
        out_shape=jax.ShapeDtypeStruct((m, d), F32),
        compiler_params=_cparams(("parallel",)),
        name="matmul_res",
    )(x, *a_list, *w_list)


def _ffn_kernel(x_ref, g_ref, wu_ref, wd_ref, gf_ref, o_ref, h_ref, acc_ref, *, final_norm):
    k = pl.program_id(1)

    @pl.when(k == 0)
    def _():
        x = x_ref[...]
        h_ref[...] = _rms(x, g_ref[...]).astype(BF16)
        acc_ref[...] = x

    u = _dot(h_ref[...], wu_ref[...])
    f = jnp.square(jnp.maximum(u, 0.0)).astype(BF16)
    acc_ref[...] += _dot(f, wd_ref[...])

    @pl.when(k == pl.num_programs(1) - 1)
    def _():
        y = acc_ref[...]
        if final_norm:
            y = _rms(y, gf_ref[...])
        o_ref[...] = y


def _ffn(x, g, wu, wd, gf, *, tm, tf, final_norm):
    m, d = x.shape
    ff = wu.shape[1]
    return pl.pallas_call(
        functools.partial(_ffn_kernel, final_norm=final_norm),
        grid=(m // tm, ff // tf),
        in_specs=[pl.BlockSpec((tm, d), lambda i, k: (i, 0)),
                  pl.BlockSpec((1, d), lambda i, k: (0, 0)),
                  pl.BlockSpec((d, tf), lambda i, k: (0, k)),
                  pl.BlockSpec((tf, d), lambda i, k: (k, 0)),
                  pl.BlockSpec((1, d), lambda i, k: (0, 0))],
        out_specs=pl.BlockSpec((tm, d), lambda i, k: (i, 0)),
        out_shape=jax.ShapeDtypeStruct((m, d), F32),
        scratch_shapes=[pltpu.VMEM((tm, d), BF16), pltpu.VMEM((tm, d), F32)],
        compiler_params=_cparams(("parallel", "arbitrary")),
        name="ffn",
    )(x, g, wu, wd, gf)


def _head_avg_matrix(width):
    r = lax.broadcasted_iota(jnp.int32, (width, width), 0) // HEAD_DIM
    c = lax.broadcasted_iota(jnp.int32, (width, width), 1) // HEAD_DIM
    return jnp.where(r == c, 1.0 / HEAD_DIM, 0.0).astype(BF16)


def _gelu_u_vn(auv, gv, aw):
    u = jax.nn.gelu(auv[:, :aw])
    v = jax.nn.gelu(auv[:, aw:])
    p = _head_avg_matrix(aw)
    mu = _split_dot(v, p, 3)
    dv = v - mu
    var = _split_dot(dv * dv, p, 3)
    return u, (dv * lax.rsqrt(var + EPS)) * gv


def _mixer_prompt_kernel(auv_ref, c3_ref, gv_ref, ws_ref, bs_ref, cw_ref, cbuf_ref,
                         ya_ref, yc_ref, cnew_ref, g_scr, *, aw, cw_width, n_heads):
    i = pl.program_id(1)
    tm = auv_ref.shape[0]

    gv = gv_ref[...]
    lane_head = lax.broadcasted_iota(jnp.int32, (CHUNK, aw), 1) // HEAD_DIM
    tri = (lax.broadcasted_iota(jnp.int32, (CHUNK, CHUNK), 0)
           >= lax.broadcasted_iota(jnp.int32, (CHUNK, CHUNK), 1))
    w_heads = [jnp.where(tri, ws_ref[h], 0.0).astype(BF16) for h in range(n_heads)]
    for c in range(tm // CHUNK):
        rows = pl.ds(c * CHUNK, CHUNK)
        u, vn = _gelu_u_vn(auv_ref[rows, :], gv, aw)
        vb = vn.astype(BF16)
        mixed = bs_ref[...]
        for h in range(n_heads):
            mixed = mixed + _dot(w_heads[h], jnp.where(lane_head == h, vb, jnp.zeros_like(vb)))
        ya_ref[rows, :] = (u * mixed).astype(ya_ref.dtype)

    @pl.when(i == 0)
    def _():
        g_scr[0:8, :] = jnp.zeros((8, cw_width), F32)
        g_scr[6:8, :] = cbuf_ref[...]

    @pl.when(i > 0)
    def _():
        g_scr[0:8, :] = g_scr[tm:tm + 8, :]

    c3 = c3_ref[...]
    gated = c3[:, cw_width:2 * cw_width] * c3[:, 2 * cw_width:]
    g_scr[8:8 + tm, :] = gated
    conv = (cw_ref[0:1, :] * g_scr[6:6 + tm, :] + cw_ref[1:2, :] * g_scr[7:7 + tm, :]
            + cw_ref[2:3, :] * gated)
    yc_ref[...] = (c3[:, :cw_width] * conv).astype(yc_ref.dtype)
    cnew_ref[...] = g_scr[tm + 6:tm + 8, :]


def _mixer_prompt(auv, c3, gv, ws, bs_full, cw, cbuf, *, nb, tm):
    m = auv.shape[0]
    aw = auv.shape[1] // 2
    cww = c3.shape[1] // 3
    n_heads = ws.shape[0]
    n_i = (m // nb) // tm
    row = lambda b, i: (b * n_i + i, 0)
    const2 = lambda b, i: (0, 0)
    return pl.pallas_call(
        functools.partial(_mixer_prompt_kernel, aw=aw, cw_width=cww, n_heads=n_heads),
        grid=(nb, n_i),
        in_specs=[pl.BlockSpec((tm, 2 * aw), row),
                  pl.BlockSpec((tm, 3 * cww), row),
                  pl.BlockSpec((1, aw), const2),
                  pl.BlockSpec(ws.shape, lambda b, i: (0, 0, 0)),
                  pl.BlockSpec(bs_full.shape, const2),
                  pl.BlockSpec(cw.shape, const2),
                  pl.BlockSpec((None, CONV_K - 1, cww), lambda b, i: (b, 0, 0))],
        out_specs=[pl.BlockSpec((tm, aw), row),
                   pl.BlockSpec((tm, cww), row),
                   pl.BlockSpec((None, CONV_K - 1, cww), lambda b, i: (b, 0, 0))],
        out_shape=[jax.ShapeDtypeStruct((m, aw), BF16),
                   jax.ShapeDtypeStruct((m, cww), BF16),
                   jax.ShapeDtypeStruct((nb, CONV_K - 1, cww), F32)],
        scratch_shapes=[pltpu.VMEM((tm + 8, cww), F32)],
        compiler_params=_cparams(("parallel", "arbitrary")),
        name="mixer_prompt",
    )(auv, c3, gv, ws, bs_full, cw, cbuf)


def _mixer_sample_kernel(auv_ref, c3_ref, gv_ref, wv_ref, bv_ref, cw_ref, cbuf_ref,
                         ya_ref, yc_ref, cnew_ref, chv_ref, *, aw, cw_width, n_t, n_seq):
    u, vn = _gelu_u_vn(auv_ref[...], gv_ref[...], aw)
    chv_ref[...] = vn
    c3 = c3_ref[...]
    gated = c3[:, cw_width:2 * cw_width] * c3[:, 2 * cw_width:]
    blk = lambda a, t: a[t * n_seq:(t + 1) * n_seq, :]
    gp = [cbuf_ref[0], cbuf_ref[1]] + [blk(gated, t) for t in range(n_t)]
    for t in range(n_t):
        mixed = bv_ref[t:t + 1, :]
        for s in range(t + 1):
            mixed = mixed + wv_ref[t * n_t + s:t * n_t + s + 1, :] * blk(vn, s)
        rows = pl.ds(t * n_seq, n_seq)
        ya_ref[rows, :] = (blk(u, t) * mixed).astype(ya_ref.dtype)
        conv = cw_ref[0:1, :] * gp[t] + cw_ref[1:2, :] * gp[t + 1] + cw_ref[2:3, :] * gp[t + 2]
        yc_ref[rows, :] = (blk(c3, t)[:, :cw_width] * conv).astype(yc_ref.dtype)
    cnew_ref[0] = gp[n_t]
    cnew_ref[1] = gp[n_t + 1]


def _mixer_sample(auv, c3, gv, wvec, bvec, cw, cbuf_t, *, n_t, n_seq):
    m = auv.shape[0]
    aw = auv.shape[1] // 2
    cww = c3.shape[1] // 3
    return pl.pallas_call(
        functools.partial(_mixer_sample_kernel, aw=aw, cw_width=cww, n_t=n_t, n_seq=n_seq),
        out_shape=[jax.ShapeDtypeStruct((m, aw), BF16),
                   jax.ShapeDtypeStruct((m, cww), BF16),
                   jax.ShapeDtypeStruct((CONV_K - 1, n_seq, cww), F32),
                   jax.ShapeDtypeStruct((m, aw), F32)],
        compiler_params=pltpu.CompilerParams(vmem_limit_bytes=VMEM_LIMIT),
        name="mixer_sample",
    )(auv, c3, gv, wvec, bvec, cw, cbuf_t)


def _sb_block(z2, mask, u_mat, carry, terms):
    lb, lr = _log2_sig_pair(z2)
    if mask is not None:
        lr = jnp.where(mask, lr, 0.0)
    between = _split_dot(lr, u_mat, terms) + carry
    a = jnp.exp2(lb + between)
    if mask is not None:
        a = jnp.where(mask, a, 0.0)
    return a, carry + jnp.sum(lr, axis=-1, keepdims=True)


def _sb_prompt_kernel(q_ref, kt_ref, vt_ref, o_ref, carry_ref, acc_ref, *, tq, tk, terms):
    qi = pl.program_id(2)
    u_mat = _strict_upper(tk)
    q = q_ref[...]
    heads = q.shape[1] // HEAD_DIM
    qh = [q[:, h * HEAD_DIM:(h + 1) * HEAD_DIM] for h in range(heads)]
    carry_ref[...] = jnp.zeros_like(carry_ref)
    acc_ref[...] = jnp.zeros_like(acc_ref)

    def step(j, r0, nr, triangular):
        rows = pl.ds(r0, nr)
        mask = None
        if triangular:
            mask = (lax.broadcasted_iota(jnp.int32, (nr, tk), 1)
                    < lax.broadcasted_iota(jnp.int32, (nr, tk), 0))
        for h in range(heads):
            hs = pl.ds(h * HEAD_DIM, HEAD_DIM)
            z = _dot(qh[h][r0:r0 + nr], kt_ref[j, hs, :])
            a, carry = _sb_block(z, mask, u_mat, carry_ref[h, rows], terms)
            carry_ref[h, rows] = carry
            acc_ref[h, rows] += _dot_nt(a.astype(BF16), vt_ref[j, hs, :])

    n_diag = tq // tk
    first_diag = qi * n_diag
    for r in range(n_diag):
        step(first_diag + r, r * tk, tk, True)
        for d in range(r - 1, -1, -1):
            step(first_diag + d, r * tk, tk, False)

    def body(n, c):
        for d in range(n_diag):
            step(first_diag - 1 - n * n_diag - d, 0, tq, False)
        return c

    lax.fori_loop(0, qi, body, 0)
    o_ref[...] = jnp.concatenate([acc_ref[h] for h in range(heads)], axis=1).astype(o_ref.dtype)


def _sb_prompt(q, kvtb, *, nb, bw, terms):
    m = q.shape[0]
    t = m // nb
    n_kb, tk = kvtb.shape[1], kvtb.shape[3]
    tq = SB_TQ
    n_q = t // tq
    gw = SB_HEADS_PER_STEP * HEAD_DIM
    n_hg = bw // gw
    return pl.pallas_call(
        functools.partial(_sb_prompt_kernel, tq=tq, tk=tk, terms=terms),
        grid=(nb, n_hg, n_q),
        in_specs=[pl.BlockSpec((tq, gw), lambda b, g, i: (b * n_q + i, g)),
                  pl.BlockSpec((None, n_kb, gw, tk), lambda b, g, i: (b, 0, g, 0)),
                  pl.BlockSpec((None, n_kb, gw, tk), lambda b, g, i: (b, 0, n_hg + g, 0))],
        out_specs=pl.BlockSpec((tq, gw), lambda b, g, i: (b * n_q + i, g)),
        out_shape=jax.ShapeDtypeStruct((m, bw), BF16),
        scratch_shapes=[pltpu.VMEM((SB_HEADS_PER_STEP, tq, 1), F32),
                        pltpu.VMEM((SB_HEADS_PER_STEP, tq, HEAD_DIM), F32)],
        compiler_params=_cparams(("parallel", "parallel", "parallel")),
        name="sb_prompt",
    )(q, kvtb, kvtb)


def _sb_sample_kernel(pt_ref, qbd_ref, kvn_ref, *rest, n_pages, n_heads, bw):
    k_refs = rest[:n_pages]
    v_refs = rest[n_pages:2 * n_pages]
    o_ref = rest[2 * n_pages]
    qbd = qbd_ref[...]
    rows = qbd.shape[0]
    page = k_refs[0].shape[1]
    n_slots = n_pages + 1
    u_mat = _strict_upper(page)

    kvn = kvn_ref[...]
    pad = jnp.zeros((page - kvn.shape[0], bw), F32)
    kn = jnp.concatenate([kvn[:, :bw], pad], axis=0).astype(BF16)
    vn = jnp.concatenate([kvn[:, bw:], pad], axis=0).astype(BF16)
    t_idx = lax.broadcasted_iota(jnp.int32, (rows, page), 0) // n_heads
    new_mask = lax.broadcasted_iota(jnp.int32, (rows, page), 1) < t_idx

    z = [_dot_nt(qbd, kn)] + [_dot(qbd, k_refs[i][...].astype(BF16)) for i in range(n_pages)]
    lb, lr = _log2_sig_pair(jnp.concatenate(z, axis=1))
    lr = [lr[:, s * page:(s + 1) * page] for s in range(n_slots)]
    lr[0] = jnp.where(new_mask, lr[0], 0.0)
    between = _dot(jnp.concatenate(lr, axis=0).astype(BF16), u_mat)
    carry = jnp.zeros((rows, 1), F32)
    x = []
    for s in range(n_slots):
        x.append(lb[:, s * page:(s + 1) * page] + between[s * rows:(s + 1) * rows] + carry)
        carry = carry + jnp.sum(lr[s], axis=-1, keepdims=True)
    a = jnp.exp2(jnp.concatenate(x, axis=1))
    a = [a[:, s * page:(s + 1) * page] for s in range(n_slots)]
    acc = _dot(jnp.where(new_mask, a[0], 0.0).astype(BF16), vn)
    for i in range(n_pages):
        acc = acc + _dot_nt(a[i + 1].astype(BF16), v_refs[i][...].astype(BF16))

    r_head = lax.broadcasted_iota(jnp.int32, acc.shape, 0) % n_heads
    l_head = lax.broadcasted_iota(jnp.int32, acc.shape, 1) // HEAD_DIM
    kept = jnp.where(r_head == l_head, acc, 0.0)
    o_ref[...] = jnp.sum(kept.reshape(rows // n_heads, n_heads, bw), axis=1)


def _sb_sample(page_table_flat, qbd, kv_new, cache_kt, cache_vt, layer, *, n_pages, n_heads):
    n_seq, rows, bw = qbd.shape
    page = cache_kt.shape[3]
    n_t = rows // n_heads

    def page_spec(i):
        def idx(b, pt):
            return (layer, pt[b * n_pages + (n_pages - 1 - i)], 0, 0)
        return pl.BlockSpec((None, None, bw, page), idx)

    seq3 = lambda b, pt: (b, 0, 0)
    grid_spec = pltpu.PrefetchScalarGridSpec(
        num_scalar_prefetch=1,
        grid=(n_seq,),
        in_specs=[pl.BlockSpec((None, rows, bw), seq3),
                  pl.BlockSpec((None,) + kv_new.shape[1:], seq3)]
                 + [page_spec(i) for i in range(n_pages)] + [page_spec(i) for i in range(n_pages)],
        out_specs=pl.BlockSpec((None, n_t, bw), seq3),
    )
    return pl.pallas_call(
        functools.partial(_sb_sample_kernel, n_pages=n_pages, n_heads=n_heads, bw=bw),
        grid_spec=grid_spec,
        out_shape=jax.ShapeDtypeStruct((n_seq, n_t, bw), F32),
        compiler_params=_cparams(("parallel",)),
        name="sb_sample",
    )(page_table_flat, qbd, kv_new, *([cache_kt] * n_pages), *([cache_vt] * n_pages))


def _mix_xattn_prompt_kernel(x_ref, ya_ref, yb_ref, yc_ref, woa_ref, wob_ref, woc_ref, g_ref, wq_ref,
                             k_ref, v_ref, wxo_ref, o_ref, att_ref, *, n_heads, hd, q_scale):
    x1 = (x_ref[...] + _dot(ya_ref[...], woa_ref[...]) + _dot(yb_ref[...], wob_ref[...])
          + _dot(yc_ref[...], woc_ref[...]))
    hq = (_dot(_rms(x1, g_ref[...]).astype(BF16), wq_ref[...]) * q_scale).astype(BF16)
    for h in range(n_heads):
        cols = pl.ds(h * hd, hd)
        s = _dot_nt(hq[:, h * hd:(h + 1) * hd], k_ref[:, cols])
        e = jnp.exp(s - jnp.max(s, axis=-1, keepdims=True))
        p = e / jnp.sum(e, axis=-1, keepdims=True)
        att_ref[:, cols] = _dot(p.astype(BF16), v_ref[:, cols]).astype(BF16)
    o_ref[...] = x1 + _dot(att_ref[...], wxo_ref[...])


def _mix_xattn_prompt(x, ys, wos, g, wq, kv, wxo, *, nb, n_mem, n_heads, tm):
    m, d = x.shape
    n_i = (m // nb) // tm
    row = lambda b, i: (b * n_i + i, 0)
    const = lambda b, i: (0, 0)
    return pl.pallas_call(
        functools.partial(_mix_xattn_prompt_kernel, n_heads=n_heads, hd=d // n_heads,
                          q_scale=(d // n_heads) ** -0.5),
        grid=(nb, n_i),
        in_specs=[pl.BlockSpec((tm, d), row)]
                 + [pl.BlockSpec((tm, y.shape[1]), row) for y in ys]
                 + [pl.BlockSpec(w.shape, const) for w in wos]
                 + [pl.BlockSpec((1, d), const), pl.BlockSpec((d, d), const),
                    pl.BlockSpec((n_mem, d), lambda b, i: (b, 0)),
                    pl.BlockSpec((n_mem, d), lambda b, i: (b, 1)),
                    pl.BlockSpec((d, d), const)],
        out_specs=pl.BlockSpec((tm, d), row),
        out_shape=jax.ShapeDtypeStruct((m, d), F32),
        scratch_shapes=[pltpu.VMEM((tm, d), BF16)],
        compiler_params=_cparams(("parallel", "parallel")),
        name="mix_xattn_prompt",
    )(x, *ys, *wos, g, wq, kv, kv, wxo)


def _xattn_sample_kernel(q_ref, k_ref, v_ref, o_ref, *, n_heads):
    n_sq, n_mem, _, hd = k_ref.shape
    for i in range(n_sq):
        k = k_ref[i].reshape(n_mem * n_heads, hd).astype(BF16)
        v = v_ref[i].reshape(n_mem * n_heads, hd).astype(BF16)
        s = _dot_nt(q_ref[i], k)
        r_head = lax.broadcasted_iota(jnp.int32, s.shape, 0) % n_heads
        c_head = lax.broadcasted_iota(jnp.int32, s.shape, 1) % n_heads
        s = jnp.where(r_head == c_head, s, -jnp.inf)
        e = jnp.exp(s - jnp.max(s, axis=-1, keepdims=True))
        p = e / jnp.sum(e, axis=-1, keepdims=True)
        o_ref[i] = _dot(p.astype(BF16), v).astype(o_ref.dtype)


def _xattn_sample(q, k, v, layer, *, n_heads):
    n_seq, rows, hd = q.shape
    n_mem = k.shape[2]
    n_sq = XATTN_SEQS_PER_STEP if n_seq % XATTN_SEQS_PER_STEP == 0 else 1
    kv_spec = pl.BlockSpec((None, n_sq, n_mem, n_heads, hd), lambda b: (layer, b, 0, 0, 0))
    return pl.pallas_call(
        functools.partial(_xattn_sample_kernel, n_heads=n_heads),
        grid=(n_seq // n_sq,),
        in_specs=[pl.BlockSpec((n_sq, rows, hd), lambda b: (b, 0, 0)), kv_spec, kv_spec],
        out_specs=pl.BlockSpec((n_sq, rows, hd), lambda b: (b, 0, 0)),
        out_shape=jax.ShapeDtypeStruct((n_seq, rows, hd), BF16),
        compiler_params=_cparams(("parallel",)),
        name="xattn_sample",
    )(q, k, v)


def _row_tile(m, cap):
    t = min(m, cap)
    assert m % t == 0
    return t


def kernel(x_prompt, x_sample, cache_sb_k, cache_sb_v, cache_mem_k, cache_mem_v, state_conv, page_table,
           mem_prompt, g_mix, w_in, b_in, g_vnorm, w_spatial, b_spatial, conv_w, w_out, g_xattn, g_mem,
           w_xq, w_xkv, w_xo, g_ffn, w_up, w_down, g_final):
    nb, seq, d = x_prompt.shape
    n_seq, n_t, _ = x_sample.shape
    depth, n_pool, page, b_heads, hd = cache_sb_k.shape
    assert hd == HEAD_DIM and page == LANES
    n_pages = page_table.shape[1]
    a_heads = g_vnorm.shape[1]
    aw = a_heads * HEAD_DIM
    bw = b_heads * HEAD_DIM
    cww = conv_w.shape[2]
    n_mem, mem_heads, mem_hd = cache_mem_k.shape[2:]
    sb_terms = 1

    mp = nb * seq
    ms = n_seq * n_t
    xp = x_prompt.reshape(mp, d)
    xs = x_sample.transpose(1, 0, 2).reshape(ms, d)
    mem = mem_prompt.reshape(nb * n_mem, d)

    cache_kt = cache_sb_k.transpose(0, 1, 3, 4, 2).reshape(depth, n_pool, bw, page)
    cache_vt = cache_sb_v.transpose(0, 1, 3, 4, 2).reshape(depth, n_pool, bw, page)
    pt_flat = page_table.reshape(-1)
    head_eye = jnp.eye(b_heads, dtype=BF16)

    row2 = lambda v: v.reshape(1, -1)
    tm_p = _row_tile(seq, 512)
    tm_s = _row_tile(ms, 512)
    ka, kb = 2 * aw + bw, 2 * aw + 3 * bw

    outs = {k: [] for k in ("kp", "vp", "ks", "vs", "cp", "cs", "chv", "mk", "mv")}
    conv_zero = jnp.zeros((nb, CONV_K - 1, cww), F32)

    for l in range(depth):
        last = l == depth - 1
        wa = w_in[l][:, :ka].astype(BF16)
        wkv = w_in[l][:, ka:kb].astype(BF16)
        wkvt = wkv.T
        wc = w_in[l][:, kb:].astype(BF16)
        ba, bkv, bc = row2(b_in[l][:ka]), row2(b_in[l][ka:kb]), row2(b_in[l][kb:])
        bkvt = bkv.reshape(-1, 1)
        wo = w_out[l].astype(BF16)
        wo_parts = [wo[:aw], wo[aw:aw + bw], wo[aw + bw:]]
        wxq, wxo = w_xq[l].astype(BF16), w_xo[l].astype(BF16)
        wu, wd = w_up[l].astype(BF16), w_down[l].astype(BF16)
        gv = g_vnorm[l].reshape(1, aw)
        bs_full = jnp.repeat(b_spatial[l].T, HEAD_DIM, axis=1)
        wvec = jnp.repeat(w_spatial[l][:, :n_t, :n_t].transpose(1, 2, 0), HEAD_DIM, axis=2).reshape(n_t * n_t, aw)
        bvec = jnp.repeat(b_spatial[l][:, :n_t].T, HEAD_DIM, axis=1)
        xq_scale = mem_hd ** -0.5

        kv_f32, kv_b = _norm_matmul(mem, row2(g_mem[l]), w_xkv[l].astype(BF16),
                                    tm=_row_tile(nb * n_mem, 512), out_dtypes=(F32, BF16))
        auv, q, c3, kt, vt, kvtb = _in_proj(xp, row2(g_mix[l]), wa, ba, wkvt, bkvt, wc, bc,
                                            nb=nb, tm=tm_p, tk=SB_TK, aw2=2 * aw, feature_major=True)
        ya, yc, cnew = _mixer_prompt(auv, c3, gv, w_spatial[l], bs_full, conv_w[l], conv_zero, nb=nb, tm=tm_p)
        yb = _sb_prompt(q, kvtb, nb=nb, bw=bw, terms=sb_terms)
        xp = _mix_xattn_prompt(xp, (ya, yb, yc), wo_parts, row2(g_xattn[l]), wxq, kv_b, wxo,
                               nb=nb, n_mem=n_mem, n_heads=mem_heads, tm=tm_p)
        xp = _ffn(xp, row2(g_ffn[l]), wu, wd, row2(g_final), tm=_row_tile(mp, 1024), tf=1024, final_norm=last)
        outs["kp"].append(kt.reshape(nb, b_heads, HEAD_DIM, seq).transpose(0, 3, 1, 2))
        outs["vp"].append(vt.reshape(nb, b_heads, HEAD_DIM, seq).transpose(0, 3, 1, 2))
        outs["cp"].append(cnew)
        outs["mk"].append(kv_f32[:, :d].reshape(nb, n_mem, mem_heads, mem_hd))
        outs["mv"].append(kv_f32[:, d:].reshape(nb, n_mem, mem_heads, mem_hd))

        auv, q, c3, kv = _in_proj(xs, row2(g_mix[l]), wa, ba, wkv, bkv, wc, bc,
                                  nb=1, tm=tm_s, tk=SB_TK, aw2=2 * aw, feature_major=False)
        ya, yc, cnew, chv = _mixer_sample(auv, c3, gv, wvec, bvec, conv_w[l],
                                          state_conv[l].transpose(1, 0, 2), n_t=n_t, n_seq=n_seq)
        q_b = q.reshape(n_t, n_seq, b_heads, HEAD_DIM).transpose(1, 0, 2, 3)
        qbd = (q_b[:, :, :, None, :] * head_eye[None, None, :, :, None]).reshape(n_seq, n_t * b_heads, bw)
        kv_b = kv.reshape(n_t, n_seq, 2 * bw).transpose(1, 0, 2)
        kv_pad = jnp.pad(kv_b, ((0, 0), (0, 8 - n_t), (0, 0)))
        yb_b = _sb_sample(pt_flat, qbd, kv_pad, cache_kt, cache_vt, l,
                          n_pages=n_pages, n_heads=b_heads)
        yb = yb_b.transpose(1, 0, 2).reshape(ms, bw).astype(BF16)
        xs = _matmul_res(xs, [ya, yb, yc], wo_parts, tm=tm_s)
        hq, = _norm_matmul(xs, row2(g_xattn[l]), wxq, tm=tm_s, out_dtypes=(BF16,), scale=xq_scale)
        hq_b = hq.reshape(n_t, n_seq, mem_heads, mem_hd).transpose(1, 0, 2, 3).reshape(n_seq, n_t * mem_heads, mem_hd)
        o_b = _xattn_sample(hq_b, cache_mem_k, cache_mem_v, l, n_heads=mem_heads)
        o = o_b.reshape(n_seq, n_t, d).transpose(1, 0, 2).reshape(ms, d)
        xs = _matmul_res(xs, [o], [wxo], tm=tm_s)
        xs = _ffn(xs, row2(g_ffn[l]), wu, wd, row2(g_final), tm=tm_s, tf=1024, final_norm=last)
        outs["ks"].append(kv_b[:, :, :bw].reshape(n_seq, n_t, b_heads, HEAD_DIM))
        outs["vs"].append(kv_b[:, :, bw:].reshape(n_seq, n_t, b_heads, HEAD_DIM))
        outs["cs"].append(cnew.transpose(1, 0, 2))
        outs["chv"].append(chv.reshape(n_t, n_seq, a_heads, HEAD_DIM).transpose(1, 0, 2, 3))

    y_prompt = xp.reshape(nb, seq, d)
    y_sample = xs.reshape(n_t, n_seq, d).transpose(1, 0, 2)
    st = lambda k: jnp.stack(outs[k])
    return (y_prompt, y_sample, st("kp"), st("vp"), st("ks"), st("vs"), st("cp"), st("cs"), st("chv"),
            st("mk"), st("mv"))
```

```python
import functools

import jax
import jax.numpy as jnp
from jax import lax
from jax.experimental import pallas as pl
from jax.experimental.pallas import tpu as pltpu

F32 = jnp.float32
BF16 = jnp.bfloat16

EPS = 1e-6
LOG2E = 1.4426950408889634
HEAD_DIM = 64
CHUNK = 128
CONV_K = 3
LANES = 128
VMEM_LIMIT = 56 * 1024 * 1024

SB_TQ = 512
SB_TK = 256
SB_HEADS_PER_STEP = 8
XATTN_SEQS_PER_STEP = 4


def _cparams(sem):
    return pltpu.CompilerParams(dimension_semantics=sem, vmem_limit_bytes=VMEM_LIMIT)


def _rms(x, g):
    ms = jnp.mean(x * x, axis=-1, keepdims=True)
    return (x * lax.rsqrt(ms + EPS)) * g


def _dot(a, b):
    return jnp.dot(a, b, preferred_element_type=F32)


def _dot_nt(a, b):
    return lax.dot_general(a, b, (((1,), (1,)), ((), ())), preferred_element_type=F32)


def _split_dot(x, m, terms):
    out = None
    r = x
    for i in range(terms):
        p = r.astype(BF16)
        d = _dot(p, m)
        out = d if out is None else out + d
        if i + 1 < terms:
            r = r - p.astype(F32)
    return out


def _log2_sig_pair(z2):
    m = jnp.minimum(z2, 0.0)
    sp = jnp.log(1.0 + jnp.exp2((m + m) - z2)) * LOG2E
    lb = m - sp
    return lb, lb - z2


def _strict_upper(n):
    r = lax.broadcasted_iota(jnp.int32, (n, n), 0)
    c = lax.broadcasted_iota(jnp.int32, (n, n), 1)
    return jnp.where(r > c, 1.0, 0.0).astype(BF16)


def _in_proj_sample_kernel(x_ref, g_ref, wa_ref, ba_ref, wkv_ref, bkv_ref, wc_ref, bc_ref,
                           auv_ref, q_ref, c3_ref, kv_ref, *, aw2, q_scale):
    h = _rms(x_ref[...], g_ref[...]).astype(BF16)
    pa = _dot(h, wa_ref[...]) + ba_ref[...]
    auv_ref[...] = pa[:, :aw2]
    q_ref[...] = (pa[:, aw2:] * q_scale).astype(BF16)
    c3_ref[...] = _dot(h, wc_ref[...]) + bc_ref[...]
    kv_ref[...] = _dot(h, wkv_ref[...]) + bkv_ref[...]


def _in_proj_sample(x, g, wa, ba, wkv, bkv, wc, bc, *, tm, aw2):
    m, d = x.shape
    na = wa.shape[1]
    bw = na - aw2
    nc = wc.shape[1]
    const = lambda i: (0, 0)
    row = lambda i: (i, 0)
    return pl.pallas_call(
        functools.partial(_in_proj_sample_kernel, aw2=aw2, q_scale=HEAD_DIM ** -0.5 * LOG2E),
        grid=(m // tm,),
        in_specs=[
            pl.BlockSpec((tm, d), row),
            pl.BlockSpec((1, d), const),
            pl.BlockSpec((d, na), const), pl.BlockSpec((1, na), const),
            pl.BlockSpec(wkv.shape, const), pl.BlockSpec(bkv.shape, const),
            pl.BlockSpec((d, nc), const), pl.BlockSpec((1, nc), const),
        ],
        out_specs=[pl.BlockSpec((tm, aw2), row), pl.BlockSpec((tm, bw), row), pl.BlockSpec((tm, nc), row),
                   pl.BlockSpec((tm, 2 * bw), row)],
        out_shape=[jax.ShapeDtypeStruct((m, aw2), F32), jax.ShapeDtypeStruct((m, bw), BF16),
                   jax.ShapeDtypeStruct((m, nc), F32), jax.ShapeDtypeStruct((m, 2 * bw), F32)],
        compiler_params=_cparams(("parallel",)),
        name="in_proj_sample",
    )(x, g, wa, ba, wkv, bkv, wc, bc)


def _norm_matmul_kernel(x_ref, g_ref, w_ref, *o_refs, scale):
    h = _rms(x_ref[...], g_ref[...]).astype(BF16)
    y = _dot(h, w_ref[...])
    if scale is not None:
        y = y * scale
    for o in o_refs:
        o[...] = y.astype(o.dtype)


def _norm_matmul(x, g, w, *, tm, out_dtypes, scale=None):
    m, d = x.shape
    n = w.shape[1]
    return pl.pallas_call(
        functools.partial(_norm_matmul_kernel, scale=scale),
        grid=(m // tm,),
        in_specs=[pl.BlockSpec((tm, d), lambda i: (i, 0)),
                  pl.BlockSpec((1, d), lambda i: (0, 0)),
                  pl.BlockSpec((d, n), lambda i: (0, 0))],
        out_specs=[pl.BlockSpec((tm, n), lambda i: (i, 0)) for _ in out_dtypes],
        out_shape=[jax.ShapeDtypeStruct((m, n), dt) for dt in out_dtypes],
        compiler_params=_cparams(("parallel",)),
        name="norm_matmul",
    )(x, g, w)


def _matmul_res_kernel(*refs, n_in):
    x_ref = refs[0]
    a_refs = refs[1:1 + n_in]
    w_refs = refs[1 + n_in:1 + 2 * n_in]
    o_ref = refs[1 + 2 * n_in]
    acc = x_ref[...]
    for a, w in zip(a_refs, w_refs):
        acc = acc + _dot(a[...], w[...])
    o_ref[...] = acc


def _matmul_res(x, a_list, w_list, *, tm):
    m, d = x.shape
    n_in = len(a_list)
    in_specs = [pl.BlockSpec((tm, d), lambda i: (i, 0))]
    in_specs += [pl.BlockSpec((tm, a.shape[1]), lambda i: (i, 0)) for a in a_list]
    in_specs += [pl.BlockSpec(w.shape, lambda i: (0, 0)) for w in w_list]
    return pl.pallas_call(
        functools.partial(_matmul_res_kernel, n_in=n_in),
        grid=(m // tm,),
        in_specs=in_specs,
        out_specs=pl.BlockSpec((tm, d), lambda i: (i, 0)),
        out_shape=jax.ShapeDtypeStruct((m, d), F32),
        compiler_params=_cparams(("parallel",)),
        name="matmul_res",
    )(x, *a_list, *w_list)


def _ffn_kernel(x_ref, g_ref, wu_ref, wd_ref, gf_ref, o_ref, h_ref, acc_ref, *, final_norm):
    k = pl.program_id(1)

    @pl.when(k == 0)
    def _():
        x = x_ref[...]
        h_ref[...] = _rms(x, g_ref[...]).astype(BF16)
        acc_ref[...] = x

    u = _dot(h_ref[...], wu_ref[...])
    f = jnp.square(jnp.maximum(u, 0.0)).astype(BF16)
    acc_ref[...] += _dot(f, wd_ref[...])

    @pl.when(k == pl.num_programs(1) - 1)
    def _():
        y = acc_ref[...]
        if final_norm:
            y = _rms(y, gf_ref[...])
        o_ref[...] = y


def _ffn(x, g, wu, wd, gf, layer, *, tm, tf, final_norm):
    m, d = x.shape
    ff = wu.shape[2]
    return pl.pallas_call(
        functools.partial(_ffn_kernel, final_norm=final_norm),
        grid=(m // tm, ff // tf),
        in_specs=[pl.BlockSpec((tm, d), lambda i, k: (i, 0)),
                  pl.BlockSpec((1, d), lambda i, k: (0, 0)),
                  pl.BlockSpec((None, d, tf), lambda i, k: (layer, 0, k)),
                  pl.BlockSpec((None, tf, d), lambda i, k: (layer, k, 0)),
                  pl.BlockSpec((1, d), lambda i, k: (0, 0))],
        out_specs=pl.BlockSpec((tm, d), lambda i, k: (i, 0)),
        out_shape=jax.ShapeDtypeStruct((m, d), F32),
        scratch_shapes=[pltpu.VMEM((tm, d), BF16), pltpu.VMEM((tm, d), F32)],
        compiler_params=_cparams(("parallel", "arbitrary")),
        name="ffn",
    )(x, g, wu, wd, gf)


def _head_avg_matrix(width):
    r = lax.broadcasted_iota(jnp.int32, (width, width), 0) // HEAD_DIM
    c = lax.broadcasted_iota(jnp.int32, (width, width), 1) // HEAD_DIM
    return jnp.where(r == c, 1.0 / HEAD_DIM, 0.0).astype(BF16)


def _gelu_u_vn(auv, gv, aw):
    u = jax.nn.gelu(auv[:, :aw])
    v = jax.nn.gelu(auv[:, aw:])
    p = _head_avg_matrix(aw)
    mu = _split_dot(v, p, 3)
    dv = v - mu
    var = _split_dot(dv * dv, p, 3)
    return u, (dv * lax.rsqrt(var + EPS)) * gv


def _mixer_prompt_rows(auv, c3, gv_ref, ws_ref, bs_ref, cw_ref, cbuf_ref, ya_ref, yc_ref, cnew_ref, g_scr):
    i = pl.program_id(1)
    tm = auv.shape[0]
    aw = auv.shape[1] // 2
    cw_width = c3.shape[1] // 3
    n_heads = ws_ref.shape[0]

    gv = gv_ref[...]
    lane_head = lax.broadcasted_iota(jnp.int32, (CHUNK, aw), 1) // HEAD_DIM
    tri = (lax.broadcasted_iota(jnp.int32, (CHUNK, CHUNK), 0)
           >= lax.broadcasted_iota(jnp.int32, (CHUNK, CHUNK), 1))
    w_heads = [jnp.where(tri, ws_ref[h], 0.0).astype(BF16) for h in range(n_heads)]
    for c in range(tm // CHUNK):
        u, vn = _gelu_u_vn(auv[c * CHUNK:(c + 1) * CHUNK], gv, aw)
        vb = vn.astype(BF16)
        mixed = bs_ref[...]
        for h in range(n_heads):
            mixed = mixed + _dot(w_heads[h], jnp.where(lane_head == h, vb, jnp.zeros_like(vb)))
        ya_ref[pl.ds(c * CHUNK, CHUNK), :] = (u * mixed).astype(ya_ref.dtype)

    @pl.when(i == 0)
    def _():
        g_scr[0:8, :] = jnp.zeros((8, cw_width), F32)
        g_scr[6:8, :] = cbuf_ref[...]

    @pl.when(i > 0)
    def _():
        g_scr[0:8, :] = g_scr[tm:tm + 8, :]

    gated = c3[:, cw_width:2 * cw_width] * c3[:, 2 * cw_width:]
    g_scr[8:8 + tm, :] = gated
    conv = (cw_ref[0:1, :] * g_scr[6:6 + tm, :] + cw_ref[1:2, :] * g_scr[7:7 + tm, :]
            + cw_ref[2:3, :] * gated)
    yc_ref[...] = (c3[:, :cw_width] * conv).astype(yc_ref.dtype)
    cnew_ref[...] = g_scr[tm + 6:tm + 8, :]


def _in_proj_prompt_kernel(x_ref, g_ref, wa_ref, ba_ref, wkv_ref, bkv_ref, wc_ref, bc_ref,
                           gv_ref, ws_ref, bs_ref, cw_ref, cbuf_ref, kt_all_hbm, vt_all_hbm,
                           q_ref, kt_ref, vt_ref, kvtb_ref, ya_ref, yc_ref, cnew_ref, g_scr,
                           *, aw2, bw, q_scale, tk):
    h = _rms(x_ref[...], g_ref[...]).astype(BF16)
    pa = _dot(h, wa_ref[...]) + ba_ref[...]
    q_ref[...] = (pa[:, aw2:] * q_scale).astype(BF16)
    kvt = _dot_nt(wkv_ref[...], h) + bkv_ref[...]
    kt_ref[...] = kvt[:bw]
    vt_ref[...] = kvt[bw:]
    for c in range(kvtb_ref.shape[0]):
        kvtb_ref[c] = kvt[:, c * tk:(c + 1) * tk].astype(BF16)
    c3 = _dot(h, wc_ref[...]) + bc_ref[...]
    _mixer_prompt_rows(pa[:, :aw2], c3, gv_ref, ws_ref, bs_ref, cw_ref, cbuf_ref,
                       ya_ref, yc_ref, cnew_ref, g_scr)


def _in_proj_prompt(x, g, wa, ba, wkvt, bkvt, wc, bc, gv, ws, bs_full, cw, cbuf, kt_all, vt_all, layer,
                    *, nb, tm, tk, aw2):
    m, d = x.shape
    t = m // nb
    n_i = t // tm
    na = wa.shape[1]
    bw = na - aw2
    aw = aw2 // 2
    nc = wc.shape[1]
    cww = nc // 3
    const = lambda b, i: (0, 0)
    row = lambda b, i: (b * n_i + i, 0)
    seq = lambda b, i: (b, 0, 0)
    return pl.pallas_call(
        functools.partial(_in_proj_prompt_kernel, aw2=aw2, bw=bw, q_scale=HEAD_DIM ** -0.5 * LOG2E, tk=tk),
        grid=(nb, n_i),
        in_specs=[
            pl.BlockSpec((tm, d), row),
            pl.BlockSpec((1, d), const),
            pl.BlockSpec((d, na), const), pl.BlockSpec((1, na), const),
            pl.BlockSpec(wkvt.shape, const), pl.BlockSpec(bkvt.shape, const),
            pl.BlockSpec((d, nc), const), pl.BlockSpec((1, nc), const),
            pl.BlockSpec((1, aw), const),
            pl.BlockSpec(ws.shape, lambda b, i: (0, 0, 0)),
            pl.BlockSpec(bs_full.shape, const),
            pl.BlockSpec(cw.shape, const),
            pl.BlockSpec((None, CONV_K - 1, cww), seq),
            pl.BlockSpec(memory_space=pl.ANY), pl.BlockSpec(memory_space=pl.ANY),
        ],
        out_specs=[
            pl.BlockSpec((tm, bw), row),
            pl.BlockSpec((None, None, bw, tm), lambda b, i: (layer, b, 0, i)),
            pl.BlockSpec((None, None, bw, tm), lambda b, i: (layer, b, 0, i)),
            pl.BlockSpec((None, tm // tk, 2 * bw, tk), lambda b, i: (b, i, 0, 0)),
            pl.BlockSpec((tm, aw), row),
            pl.BlockSpec((tm, cww), row),
            pl.BlockSpec((None, CONV_K - 1, cww), seq),
        ],
        out_shape=[
            jax.ShapeDtypeStruct((m, bw), BF16),
            jax.ShapeDtypeStruct(kt_all.shape, F32), jax.ShapeDtypeStruct(vt_all.shape, F32),
            jax.ShapeDtypeStruct((nb, t // tk, 2 * bw, tk), BF16),
            jax.ShapeDtypeStruct((m, aw), BF16),
            jax.ShapeDtypeStruct((m, cww), BF16),
            jax.ShapeDtypeStruct((nb, CONV_K - 1, cww), F32),
        ],
        input_output_aliases={13: 1, 14: 2},
        scratch_shapes=[pltpu.VMEM((tm + 8, cww), F32)],
        compiler_params=_cparams(("parallel", "arbitrary")),
        name="in_proj_prompt",
    )(x, g, wa, ba, wkvt, bkvt, wc, bc, gv, ws, bs_full, cw, cbuf, kt_all, vt_all)


def _mixer_sample_kernel(auv_ref, c3_ref, gv_ref, wv_ref, bv_ref, cw_ref, cbuf_ref,
                         ya_ref, yc_ref, cnew_ref, chv_ref, *, aw, cw_width, n_t, n_seq):
    u, vn = _gelu_u_vn(auv_ref[...], gv_ref[...], aw)
    chv_ref[...] = vn
    c3 = c3_ref[...]
    gated = c3[:, cw_width:2 * cw_width] * c3[:, 2 * cw_width:]
    blk = lambda a, t: a[t * n_seq:(t + 1) * n_seq, :]
    gp = [cbuf_ref[0], cbuf_ref[1]] + [blk(gated, t) for t in range(n_t)]
    for t in range(n_t):
        mixed = bv_ref[t:t + 1, :]
        for s in range(t + 1):
            mixed = mixed + wv_ref[t * n_t + s:t * n_t + s + 1, :] * blk(vn, s)
        rows = pl.ds(t * n_seq, n_seq)
        ya_ref[rows, :] = (blk(u, t) * mixed).astype(ya_ref.dtype)
        conv = cw_ref[0:1, :] * gp[t] + cw_ref[1:2, :] * gp[t + 1] + cw_ref[2:3, :] * gp[t + 2]
        yc_ref[rows, :] = (blk(c3, t)[:, :cw_width] * conv).astype(yc_ref.dtype)
    cnew_ref[0] = gp[n_t]
    cnew_ref[1] = gp[n_t + 1]


def _mixer_sample(auv, c3, gv, wvec, bvec, cw, cbuf_t, *, n_t, n_seq):
    m = auv.shape[0]
    aw = auv.shape[1] // 2
    cww = c3.shape[1] // 3
    return pl.pallas_call(
        functools.partial(_mixer_sample_kernel, aw=aw, cw_width=cww, n_t=n_t, n_seq=n_seq),
        out_shape=[jax.ShapeDtypeStruct((m, aw), BF16),
                   jax.ShapeDtypeStruct((m, cww), BF16),
                   jax.ShapeDtypeStruct((CONV_K - 1, n_seq, cww), F32),
                   jax.ShapeDtypeStruct((m, aw), F32)],
        compiler_params=pltpu.CompilerParams(vmem_limit_bytes=VMEM_LIMIT),
        name="mixer_sample",
    )(auv, c3, gv, wvec, bvec, cw, cbuf_t)


def _sb_block(z2, mask, u_mat, carry, terms):
    lb, lr = _log2_sig_pair(z2)
    if mask is not None:
        lr = jnp.where(mask, lr, 0.0)
    between = _split_dot(lr, u_mat, terms) + carry
    a = jnp.exp2(lb + between)
    if mask is not None:
        a = jnp.where(mask, a, 0.0)
    return a, carry + jnp.sum(lr, axis=-1, keepdims=True)


def _sb_prompt_kernel(q_ref, kt_ref, vt_ref, o_ref, carry_ref, acc_ref, *, tq, tk, terms):
    qi = pl.program_id(2)
    u_mat = _strict_upper(tk)
    q = q_ref[...]
    heads = q.shape[1] // HEAD_DIM
    qh = [q[:, h * HEAD_DIM:(h + 1) * HEAD_DIM] for h in range(heads)]
    carry_ref[...] = jnp.zeros_like(carry_ref)
    acc_ref[...] = jnp.zeros_like(acc_ref)

    def step(j, r0, nr, triangular):
        rows = pl.ds(r0, nr)
        mask = None
        if triangular:
            mask = (lax.broadcasted_iota(jnp.int32, (nr, tk), 1)
                    < lax.broadcasted_iota(jnp.int32, (nr, tk), 0))
        for h in range(heads):
            hs = pl.ds(h * HEAD_DIM, HEAD_DIM)
            z = _dot(qh[h][r0:r0 + nr], kt_ref[j, hs, :])
            a, carry = _sb_block(z, mask, u_mat, carry_ref[h, rows], terms)
            carry_ref[h, rows] = carry
            acc_ref[h, rows] += _dot_nt(a.astype(BF16), vt_ref[j, hs, :])

    n_diag = tq // tk
    first_diag = qi * n_diag
    for r in range(n_diag):
        step(first_diag + r, r * tk, tk, True)
        for d in range(r - 1, -1, -1):
            step(first_diag + d, r * tk, tk, False)

    def body(n, c):
        for d in range(n_diag):
            step(first_diag - 1 - n * n_diag - d, 0, tq, False)
        return c

    lax.fori_loop(0, qi, body, 0)
    o_ref[...] = jnp.concatenate([acc_ref[h] for h in range(heads)], axis=1).astype(o_ref.dtype)


def _sb_prompt(q, kvtb, *, nb, bw, terms):
    m = q.shape[0]
    t = m // nb
    n_kb, tk = kvtb.shape[1], kvtb.shape[3]
    tq = SB_TQ
    n_q = t // tq
    gw = SB_HEADS_PER_STEP * HEAD_DIM
    n_hg = bw // gw
    return pl.pallas_call(
        functools.partial(_sb_prompt_kernel, tq=tq, tk=tk, terms=terms),
        grid=(nb, n_hg, n_q),
        in_specs=[pl.BlockSpec((tq, gw), lambda b, g, i: (b * n_q + i, g)),
                  pl.BlockSpec((None, n_kb, gw, tk), lambda b, g, i: (b, 0, g, 0)),
                  pl.BlockSpec((None, n_kb, gw, tk), lambda b, g, i: (b, 0, n_hg + g, 0))],
        out_specs=pl.BlockSpec((tq, gw), lambda b, g, i: (b * n_q + i, g)),
        out_shape=jax.ShapeDtypeStruct((m, bw), BF16),
        scratch_shapes=[pltpu.VMEM((SB_HEADS_PER_STEP, tq, 1), F32),
                        pltpu.VMEM((SB_HEADS_PER_STEP, tq, HEAD_DIM), F32)],
        compiler_params=_cparams(("parallel", "parallel", "parallel")),
        name="sb_prompt",
    )(q, kvtb, kvtb)


def _sb_sample_kernel(pt_ref, qbd_ref, kvn_ref, *rest, n_pages, n_heads, bw):
    k_refs = rest[:n_pages]
    v_refs = rest[n_pages:2 * n_pages]
    o_ref = rest[2 * n_pages]
    qbd = qbd_ref[...]
    rows = qbd.shape[0]
    page = k_refs[0].shape[1]
    n_slots = n_pages + 1
    u_mat = _strict_upper(page)

    kvn = kvn_ref[...]
    pad = jnp.zeros((page - kvn.shape[0], bw), F32)
    kn = jnp.concatenate([kvn[:, :bw], pad], axis=0).astype(BF16)
    vn = jnp.concatenate([kvn[:, bw:], pad], axis=0).astype(BF16)
    t_idx = lax.broadcasted_iota(jnp.int32, (rows, page), 0) // n_heads
    new_mask = lax.broadcasted_iota(jnp.int32, (rows, page), 1) < t_idx

    z = [_dot_nt(qbd, kn)] + [_dot(qbd, k_refs[i][...].astype(BF16)) for i in range(n_pages)]
    lb, lr = _log2_sig_pair(jnp.concatenate(z, axis=1))
    lr = [lr[:, s * page:(s + 1) * page] for s in range(n_slots)]
    lr[0] = jnp.where(new_mask, lr[0], 0.0)
    between = _dot(jnp.concatenate(lr, axis=0).astype(BF16), u_mat)
    carry = jnp.zeros((rows, 1), F32)
    x = []
    for s in range(n_slots):
        x.append(lb[:, s * page:(s + 1) * page] + between[s * rows:(s + 1) * rows] + carry)
        carry = carry + jnp.sum(lr[s], axis=-1, keepdims=True)
    a = jnp.exp2(jnp.concatenate(x, axis=1))
    a = [a[:, s * page:(s + 1) * page] for s in range(n_slots)]
    acc = _dot(jnp.where(new_mask, a[0], 0.0).astype(BF16), vn)
    for i in range(n_pages):
        acc = acc + _dot_nt(a[i + 1].astype(BF16), v_refs[i][...].astype(BF16))

    r_head = lax.broadcasted_iota(jnp.int32, acc.shape, 0) % n_heads
    l_head = lax.broadcasted_iota(jnp.int32, acc.shape, 1) // HEAD_DIM
    kept = jnp.where(r_head == l_head, acc, 0.0)
    o_ref[...] = jnp.sum(kept.reshape(rows // n_heads, n_heads, bw), axis=1)


def _sb_sample(page_table_flat, qbd, kv_new, cache_kt, cache_vt, layer, *, n_pages, n_heads):
    n_seq, rows, bw = qbd.shape
    page = cache_kt.shape[3]
    n_t = rows // n_heads

    def page_spec(i):
        def idx(b, pt):
            return (layer, pt[b * n_pages + (n_pages - 1 - i)], 0, 0)
        return pl.BlockSpec((None, None, bw, page), idx)

    seq3 = lambda b, pt: (b, 0, 0)
    grid_spec = pltpu.PrefetchScalarGridSpec(
        num_scalar_prefetch=1,
        grid=(n_seq,),
        in_specs=[pl.BlockSpec((None, rows, bw), seq3),
                  pl.BlockSpec((None,) + kv_new.shape[1:], seq3)]
                 + [page_spec(i) for i in range(n_pages)] + [page_spec(i) for i in range(n_pages)],
        out_specs=pl.BlockSpec((None, n_t, bw), seq3),
    )
    return pl.pallas_call(
        functools.partial(_sb_sample_kernel, n_pages=n_pages, n_heads=n_heads, bw=bw),
        grid_spec=grid_spec,
        out_shape=jax.ShapeDtypeStruct((n_seq, n_t, bw), F32),
        compiler_params=_cparams(("parallel",)),
        name="sb_sample",
    )(page_table_flat, qbd, kv_new, *([cache_kt] * n_pages), *([cache_vt] * n_pages))


def _mix_xattn_prompt_kernel(x_ref, ya_ref, yb_ref, yc_ref, woa_ref, wob_ref, woc_ref, g_ref, wq_ref,
                             k_ref, v_ref, wxo_ref, o_ref, att_ref, *, n_heads, hd, q_scale):
    x1 = (x_ref[...] + _dot(ya_ref[...], woa_ref[...]) + _dot(yb_ref[...], wob_ref[...])
          + _dot(yc_ref[...], woc_ref[...]))
    hq = (_dot(_rms(x1, g_ref[...]).astype(BF16), wq_ref[...]) * q_scale).astype(BF16)
    for h in range(n_heads):
        cols = pl.ds(h * hd, hd)
        s = _dot_nt(hq[:, h * hd:(h + 1) * hd], k_ref[:, cols])
        e = jnp.exp(s - jnp.max(s, axis=-1, keepdims=True))
        p = e / jnp.sum(e, axis=-1, keepdims=True)
        att_ref[:, cols] = _dot(p.astype(BF16), v_ref[:, cols]).astype(BF16)
    o_ref[...] = x1 + _dot(att_ref[...], wxo_ref[...])


def _mix_xattn_prompt(x, ys, wos, g, wq, kv, wxo, *, nb, n_mem, n_heads, tm):
    m, d = x.shape
    n_i = (m // nb) // tm
    row = lambda b, i: (b * n_i + i, 0)
    const = lambda b, i: (0, 0)
    return pl.pallas_call(
        functools.partial(_mix_xattn_prompt_kernel, n_heads=n_heads, hd=d // n_heads,
                          q_scale=(d // n_heads) ** -0.5),
        grid=(nb, n_i),
        in_specs=[pl.BlockSpec((tm, d), row)]
                 + [pl.BlockSpec((tm, y.shape[1]), row) for y in ys]
                 + [pl.BlockSpec(w.shape, const) for w in wos]
                 + [pl.BlockSpec((1, d), const), pl.BlockSpec((d, d), const),
                    pl.BlockSpec((n_mem, d), lambda b, i: (b, 0)),
                    pl.BlockSpec((n_mem, d), lambda b, i: (b, 1)),
                    pl.BlockSpec((d, d), const)],
        out_specs=pl.BlockSpec((tm, d), row),
        out_shape=jax.ShapeDtypeStruct((m, d), F32),
        scratch_shapes=[pltpu.VMEM((tm, d), BF16)],
        compiler_params=_cparams(("parallel", "parallel")),
        name="mix_xattn_prompt",
    )(x, *ys, *wos, g, wq, kv, kv, wxo)


def _xattn_sample_kernel(q_ref, k_ref, v_ref, o_ref, *, n_heads):
    n_sq, n_mem, _, hd = k_ref.shape
    for i in range(n_sq):
        k = k_ref[i].reshape(n_mem * n_heads, hd).astype(BF16)
        v = v_ref[i].reshape(n_mem * n_heads, hd).astype(BF16)
        s = _dot_nt(q_ref[i], k)
        r_head = lax.broadcasted_iota(jnp.int32, s.shape, 0) % n_heads
        c_head = lax.broadcasted_iota(jnp.int32, s.shape, 1) % n_heads
        s = jnp.where(r_head == c_head, s, -jnp.inf)
        e = jnp.exp(s - jnp.max(s, axis=-1, keepdims=True))
        p = e / jnp.sum(e, axis=-1, keepdims=True)
        o_ref[i] = _dot(p.astype(BF16), v).astype(o_ref.dtype)


def _xattn_sample(q, k, v, layer, *, n_heads):
    n_seq, rows, hd = q.shape
    n_mem = k.shape[2]
    n_sq = XATTN_SEQS_PER_STEP if n_seq % XATTN_SEQS_PER_STEP == 0 else 1
    kv_spec = pl.BlockSpec((None, n_sq, n_mem, n_heads, hd), lambda b: (layer, b, 0, 0, 0))
    return pl.pallas_call(
        functools.partial(_xattn_sample_kernel, n_heads=n_heads),
        grid=(n_seq // n_sq,),
        in_specs=[pl.BlockSpec((n_sq, rows, hd), lambda b: (b, 0, 0)), kv_spec, kv_spec],
        out_specs=pl.BlockSpec((n_sq, rows, hd), lambda b: (b, 0, 0)),
        out_shape=jax.ShapeDtypeStruct((n_seq, rows, hd), BF16),
        compiler_params=_cparams(("parallel",)),
        name="xattn_sample",
    )(q, k, v)


def _row_tile(m, cap):
    t = min(m, cap)
    assert m % t == 0
    return t


def kernel(x_prompt, x_sample, cache_sb_k, cache_sb_v, cache_mem_k, cache_mem_v, state_conv, page_table,
           mem_prompt, g_mix, w_in, b_in, g_vnorm, w_spatial, b_spatial, conv_w, w_out, g_xattn, g_mem,
           w_xq, w_xkv, w_xo, g_ffn, w_up, w_down, g_final):
    nb, seq, d = x_prompt.shape
    n_seq, n_t, _ = x_sample.shape
    depth, n_pool, page, b_heads, hd = cache_sb_k.shape
    assert hd == HEAD_DIM and page == LANES
    n_pages = page_table.shape[1]
    a_heads = g_vnorm.shape[1]
    aw = a_heads * HEAD_DIM
    bw = b_heads * HEAD_DIM
    cww = conv_w.shape[2]
    n_mem, mem_heads, mem_hd = cache_mem_k.shape[2:]
    sb_terms = 1

    mp = nb * seq
    ms = n_seq * n_t
    xp = x_prompt.reshape(mp, d)
    xs = x_sample.transpose(1, 0, 2).reshape(ms, d)
    mem = mem_prompt.reshape(nb * n_mem, d)

    cache_kt = cache_sb_k.transpose(0, 1, 3, 4, 2).reshape(depth, n_pool, bw, page)
    cache_vt = cache_sb_v.transpose(0, 1, 3, 4, 2).reshape(depth, n_pool, bw, page)
    pt_flat = page_table.reshape(-1)
    head_eye = jnp.eye(b_heads, dtype=BF16)
    wu_all, wd_all = w_up.astype(BF16), w_down.astype(BF16)

    row2 = lambda v: v.reshape(1, -1)
    tm_p = _row_tile(seq, 512)
    tm_s = _row_tile(ms, 512)
    ka, kb = 2 * aw + bw, 2 * aw + 3 * bw

    outs = {k: [] for k in ("ks", "vs", "cp", "cs", "chv", "mk", "mv")}
    conv_zero = jnp.zeros((nb, CONV_K - 1, cww), F32)
    kt_all = jnp.zeros((depth, nb, bw, seq), F32)
    vt_all = jnp.zeros((depth, nb, bw, seq), F32)

    for l in range(depth):
        last = l == depth - 1
        wa = w_in[l][:, :ka].astype(BF16)
        wkv = w_in[l][:, ka:kb].astype(BF16)
        wkvt = wkv.T
        wc = w_in[l][:, kb:].astype(BF16)
        ba, bkv, bc = row2(b_in[l][:ka]), row2(b_in[l][ka:kb]), row2(b_in[l][kb:])
        bkvt = bkv.reshape(-1, 1)
        wo = w_out[l].astype(BF16)
        wo_parts = [wo[:aw], wo[aw:aw + bw], wo[aw + bw:]]
        wxq, wxo = w_xq[l].astype(BF16), w_xo[l].astype(BF16)
        gv = g_vnorm[l].reshape(1, aw)
        bs_full = jnp.repeat(b_spatial[l].T, HEAD_DIM, axis=1)
        wvec = jnp.repeat(w_spatial[l][:, :n_t, :n_t].transpose(1, 2, 0), HEAD_DIM, axis=2).reshape(n_t * n_t, aw)
        bvec = jnp.repeat(b_spatial[l][:, :n_t].T, HEAD_DIM, axis=1)
        xq_scale = mem_hd ** -0.5

        kv_f32, kv_b = _norm_matmul(mem, row2(g_mem[l]), w_xkv[l].astype(BF16),
                                    tm=_row_tile(nb * n_mem, 512), out_dtypes=(F32, BF16))
        q, kt_all, vt_all, kvtb, ya, yc, cnew = _in_proj_prompt(
            xp, row2(g_mix[l]), wa, ba, wkvt, bkvt, wc, bc, gv, w_spatial[l], bs_full, conv_w[l], conv_zero,
            kt_all, vt_all, l, nb=nb, tm=tm_p, tk=SB_TK, aw2=2 * aw)
        yb = _sb_prompt(q, kvtb, nb=nb, bw=bw, terms=sb_terms)
        xp = _mix_xattn_prompt(xp, (ya, yb, yc), wo_parts, row2(g_xattn[l]), wxq, kv_b, wxo,
                               nb=nb, n_mem=n_mem, n_heads=mem_heads, tm=tm_p)
        xp = _ffn(xp, row2(g_ffn[l]), wu_all, wd_all, row2(g_final), l,
                  tm=_row_tile(mp, 1024), tf=1024, final_norm=last)
        outs["cp"].append(cnew)
        outs["mk"].append(kv_f32[:, :d].reshape(nb, n_mem, mem_heads, mem_hd))
        outs["mv"].append(kv_f32[:, d:].reshape(nb, n_mem, mem_heads, mem_hd))

        auv, q, c3, kv = _in_proj_sample(xs, row2(g_mix[l]), wa, ba, wkv, bkv, wc, bc, tm=tm_s, aw2=2 * aw)
        ya, yc, cnew, chv = _mixer_sample(auv, c3, gv, wvec, bvec, conv_w[l],
                                          state_conv[l].transpose(1, 0, 2), n_t=n_t, n_seq=n_seq)
        q_b = q.reshape(n_t, n_seq, b_heads, HEAD_DIM).transpose(1, 0, 2, 3)
        qbd = (q_b[:, :, :, None, :] * head_eye[None, None, :, :, None]).reshape(n_seq, n_t * b_heads, bw)
        kv_b = kv.reshape(n_t, n_seq, 2 * bw).transpose(1, 0, 2)
        kv_pad = jnp.pad(kv_b, ((0, 0), (0, 8 - n_t), (0, 0)))
        yb_b = _sb_sample(pt_flat, qbd, kv_pad, cache_kt, cache_vt, l,
                          n_pages=n_pages, n_heads=b_heads)
        yb = yb_b.transpose(1, 0, 2).reshape(ms, bw).astype(BF16)
        xs = _matmul_res(xs, [ya, yb, yc], wo_parts, tm=tm_s)
        hq, = _norm_matmul(xs, row2(g_xattn[l]), wxq, tm=tm_s, out_dtypes=(BF16,), scale=xq_scale)
        hq_b = hq.reshape(n_t, n_seq, mem_heads, mem_hd).transpose(1, 0, 2, 3).reshape(n_seq, n_t * mem_heads, mem_hd)
        o_b = _xattn_sample(hq_b, cache_mem_k, cache_mem_v, l, n_heads=mem_heads)
        o = o_b.reshape(n_seq, n_t, d).transpose(1, 0, 2).reshape(ms, d)
        xs = _matmul_res(xs, [o], [wxo], tm=tm_s)
        xs = _ffn(xs, row2(g_ffn[l]), wu_all, wd_all, row2(g_final), l, tm=tm_s, tf=1024, final_norm=last)
        outs["ks"].append(kv_b[:, :, :bw].reshape(n_seq, n_t, b_heads, HEAD_DIM))
        outs["vs"].append(kv_b[:, :, bw:].reshape(n_seq, n_t, b_heads, HEAD_DIM))
        outs["cs"].append(cnew.transpose(1, 0, 2))
        outs["chv"].append(chv.reshape(n_t, n_seq, a_heads, HEAD_DIM).transpose(1, 0, 2, 3))

    y_prompt = xp.reshape(nb, seq, d)
    y_sample = xs.reshape(n_t, n_seq, d).transpose(1, 0, 2)
    st = lambda k: jnp.stack(outs[k])
    kp = kt_all.reshape(depth, nb, b_heads, HEAD_DIM, seq).transpose(0, 1, 4, 2, 3)
    vp = vt_all.reshape(depth, nb, b_heads, HEAD_DIM, seq).transpose(0, 1, 4, 2, 3)
    return (y_prompt, y_sample, kp, vp, st("ks"), st("vs"), st("cp"), st("cs"), st("chv"),
            st("mk"), st("mv"))
```

```python
import functools

import jax
import jax.numpy as jnp
from jax import lax
from jax.experimental import pallas as pl
from jax.experimental.pallas import tpu as pltpu

F32 = jnp.float32
BF16 = jnp.bfloat16

EPS = 1e-6
LOG2E = 1.4426950408889634
HEAD_DIM = 64
CHUNK = 128
CONV_K = 3
LANES = 128
VMEM_LIMIT = 56 * 1024 * 1024

SB_TQ = 512
SB_TK = 256
SB_HEADS_PER_STEP = 8
XATTN_SEQS_PER_STEP = 4


def _cparams(sem):
    return pltpu.CompilerParams(dimension_semantics=sem, vmem_limit_bytes=VMEM_LIMIT)


def _rms(x, g):
    ms = jnp.mean(x * x, axis=-1, keepdims=True)
    return (x * lax.rsqrt(ms + EPS)) * g


def _dot(a, b):
    return jnp.dot(a, b, preferred_element_type=F32)


def _dot_nt(a, b):
    return lax.dot_general(a, b, (((1,), (1,)), ((), ())), preferred_element_type=F32)


def _split_dot(x, m, terms):
    out = None
    r = x
    for i in range(terms):
        p = r.astype(BF16)
        d = _dot(p, m)
        out = d if out is None else out + d
        if i + 1 < terms:
            r = r - p.astype(F32)
    return out


def _log2_sig_pair(z2):
    m = jnp.minimum(z2, 0.0)
    sp = jnp.log(1.0 + jnp.exp2((m + m) - z2)) * LOG2E
    lb = m - sp
    return lb, lb - z2


def _strict_upper(n):
    r = lax.broadcasted_iota(jnp.int32, (n, n), 0)
    c = lax.broadcasted_iota(jnp.int32, (n, n), 1)
    return jnp.where(r > c, 1.0, 0.0).astype(BF16)


def _in_proj_sample_kernel(x_ref, g_ref, wa_ref, ba_ref, wkv_ref, bkv_ref, wc_ref, bc_ref,
                           auv_ref, q_ref, c3_ref, kv_ref, *, aw2, q_scale):
    h = _rms(x_ref[...], g_ref[...]).astype(BF16)
    pa = _dot(h, wa_ref[...]) + ba_ref[...]
    auv_ref[...] = pa[:, :aw2]
    q_ref[...] = (pa[:, aw2:] * q_scale).astype(BF16)
    c3_ref[...] = _dot(h, wc_ref[...]) + bc_ref[...]
    kv_ref[...] = _dot(h, wkv_ref[...]) + bkv_ref[...]


def _in_proj_sample(x, g, wa, ba, wkv, bkv, wc, bc, *, tm, aw2):
    m, d = x.shape
    na = wa.shape[1]
    bw = na - aw2
    nc = wc.shape[1]
    const = lambda i: (0, 0)
    row = lambda i: (i, 0)
    return pl.pallas_call(
        functools.partial(_in_proj_sample_kernel, aw2=aw2, q_scale=HEAD_DIM ** -0.5 * LOG2E),
        grid=(m // tm,),
        in_specs=[
            pl.BlockSpec((tm, d), row),
            pl.BlockSpec((1, d), const),
            pl.BlockSpec((d, na), const), pl.BlockSpec((1, na), const),
            pl.BlockSpec(wkv.shape, const), pl.BlockSpec(bkv.shape, const),
            pl.BlockSpec((d, nc), const), pl.BlockSpec((1, nc), const),
        ],
        out_specs=[pl.BlockSpec((tm, aw2), row), pl.BlockSpec((tm, bw), row), pl.BlockSpec((tm, nc), row),
                   pl.BlockSpec((tm, 2 * bw), row)],
        out_shape=[jax.ShapeDtypeStruct((m, aw2), F32), jax.ShapeDtypeStruct((m, bw), BF16),
                   jax.ShapeDtypeStruct((m, nc), F32), jax.ShapeDtypeStruct((m, 2 * bw), F32)],
        compiler_params=_cparams(("parallel",)),
        name="in_proj_sample",
    )(x, g, wa, ba, wkv, bkv, wc, bc)


def _norm_matmul_kernel(x_ref, g_ref, w_ref, *o_refs, scale):
    h = _rms(x_ref[...], g_ref[...]).astype(BF16)
    y = _dot(h, w_ref[...])
    if scale is not None:
        y = y * scale
    for o in o_refs:
        o[...] = y.astype(o.dtype)


def _norm_matmul(x, g, w, *, tm, out_dtypes, scale=None):
    m, d = x.shape
    n = w.shape[1]
    return pl.pallas_call(
        functools.partial(_norm_matmul_kernel, scale=scale),
        grid=(m // tm,),
        in_specs=[pl.BlockSpec((tm, d), lambda i: (i, 0)),
                  pl.BlockSpec((1, d), lambda i: (0, 0)),
                  pl.BlockSpec((d, n), lambda i: (0, 0))],
        out_specs=[pl.BlockSpec((tm, n), lambda i: (i, 0)) for _ in out_dtypes],
        out_shape=[jax.ShapeDtypeStruct((m, n), dt) for dt in out_dtypes],
        compiler_params=_cparams(("parallel",)),
        name="norm_matmul",
    )(x, g, w)


def _matmul_res_kernel(*refs, n_in):
    x_ref = refs[0]
    a_refs = refs[1:1 + n_in]
    w_refs = refs[1 + n_in:1 + 2 * n_in]
    o_ref = refs[1 + 2 * n_in]
    acc = x_ref[...]
    for a, w in zip(a_refs, w_refs):
        acc = acc + _dot(a[...], w[...])
    o_ref[...] = acc


def _matmul_res(x, a_list, w_list, *, tm):
    m, d = x.shape
    n_in = len(a_list)
    in_specs = [pl.BlockSpec((tm, d), lambda i: (i, 0))]
    in_specs += [pl.BlockSpec((tm, a.shape[1]), lambda i: (i, 0)) for a in a_list]
    in_specs += [pl.BlockSpec(w.shape, lambda i: (0, 0)) for w in w_list]
    return pl.pallas_call(
        functools.partial(_matmul_res_kernel, n_in=n_in),
        grid=(m // tm,),
        in_specs=in_specs,
        out_specs=pl.BlockSpec((tm, d), lambda i: (i, 0)),
        out_shape=jax.ShapeDtypeStruct((m, d), F32),
        compiler_params=_cparams(("parallel",)),
        name="matmul_res",
    )(x, *a_list, *w_list)


def _ffn_kernel(x_ref, g_ref, wu_ref, wd_ref, gf_ref, o_ref, h_ref, acc_ref, *, final_norm):
    k = pl.program_id(1)

    @pl.when(k == 0)
    def _():
        x = x_ref[...]
        h_ref[...] = _rms(x, g_ref[...]).astype(BF16)
        acc_ref[...] = x

    u = _dot(h_ref[...], wu_ref[...])
    f = jnp.square(jnp.maximum(u, 0.0)).astype(BF16)
    acc_ref[...] += _dot(f, wd_ref[...])

    @pl.when(k == pl.num_programs(1) - 1)
    def _():
        y = acc_ref[...]
        if final_norm:
            y = _rms(y, gf_ref[...])
        o_ref[...] = y


def _ffn(x, g, wu, wd, gf, layer, *, tm, tf, final_norm):
    m, d = x.shape
    ff = wu.shape[2]
    return pl.pallas_call(
        functools.partial(_ffn_kernel, final_norm=final_norm),
        grid=(m // tm, ff // tf),
        in_specs=[pl.BlockSpec((tm, d), lambda i, k: (i, 0)),
                  pl.BlockSpec((1, d), lambda i, k: (0, 0)),
                  pl.BlockSpec((None, d, tf), lambda i, k: (layer, 0, k)),
                  pl.BlockSpec((None, tf, d), lambda i, k: (layer, k, 0)),
                  pl.BlockSpec((1, d), lambda i, k: (0, 0))],
        out_specs=pl.BlockSpec((tm, d), lambda i, k: (i, 0)),
        out_shape=jax.ShapeDtypeStruct((m, d), F32),
        scratch_shapes=[pltpu.VMEM((tm, d), BF16), pltpu.VMEM((tm, d), F32)],
        compiler_params=_cparams(("parallel", "arbitrary")),
        name="ffn",
    )(x, g, wu, wd, gf)


def _head_avg_matrix(width):
    r = lax.broadcasted_iota(jnp.int32, (width, width), 0) // HEAD_DIM
    c = lax.broadcasted_iota(jnp.int32, (width, width), 1) // HEAD_DIM
    return jnp.where(r == c, 1.0 / HEAD_DIM, 0.0).astype(BF16)


def _gelu_u_vn(auv, gv, aw):
    u = jax.nn.gelu(auv[:, :aw])
    v = jax.nn.gelu(auv[:, aw:])
    p = _head_avg_matrix(aw)
    mu = _split_dot(v, p, 3)
    dv = v - mu
    var = _split_dot(dv * dv, p, 3)
    return u, (dv * lax.rsqrt(var + EPS)) * gv


def _mixer_prompt_rows(auv, c3, r0, gv, w_heads, bs_ref, cw_ref, ya_ref, yc_ref, g_scr):
    nr = auv.shape[0]
    aw = auv.shape[1] // 2
    cw_width = c3.shape[1] // 3

    lane_head = lax.broadcasted_iota(jnp.int32, (CHUNK, aw), 1) // HEAD_DIM
    for c in range(nr // CHUNK):
        u, vn = _gelu_u_vn(auv[c * CHUNK:(c + 1) * CHUNK], gv, aw)
        vb = vn.astype(BF16)
        mixed = bs_ref[...]
        for h, w in enumerate(w_heads):
            mixed = mixed + _dot(w, jnp.where(lane_head == h, vb, jnp.zeros_like(vb)))
        ya_ref[pl.ds(r0 + c * CHUNK, CHUNK), :] = (u * mixed).astype(ya_ref.dtype)

    gated = c3[:, cw_width:2 * cw_width] * c3[:, 2 * cw_width:]
    g_scr[8 + r0:8 + r0 + nr, :] = gated
    conv = (cw_ref[0:1, :] * g_scr[6 + r0:6 + r0 + nr, :] + cw_ref[1:2, :] * g_scr[7 + r0:7 + r0 + nr, :]
            + cw_ref[2:3, :] * gated)
    yc_ref[pl.ds(r0, nr), :] = (c3[:, :cw_width] * conv).astype(yc_ref.dtype)


def _in_proj_prompt_kernel(x_ref, g_ref, wa_ref, ba_ref, wkv_ref, bkv_ref, wc_ref, bc_ref,
                           gv_ref, ws_ref, bs_ref, cw_ref, cbuf_ref, kt_all_hbm, vt_all_hbm,
                           q_ref, kt_ref, vt_ref, kvtb_ref, ya_ref, yc_ref, cnew_ref, g_scr,
                           *, aw2, bw, q_scale, tk):
    i = pl.program_id(1)
    tm = x_ref.shape[0]
    cw_width = g_scr.shape[1]

    @pl.when(i == 0)
    def _():
        g_scr[0:8, :] = jnp.zeros((8, cw_width), F32)
        g_scr[6:8, :] = cbuf_ref[...]

    @pl.when(i > 0)
    def _():
        g_scr[0:8, :] = g_scr[tm:tm + 8, :]

    gv = gv_ref[...]
    tri = (lax.broadcasted_iota(jnp.int32, (CHUNK, CHUNK), 0)
           >= lax.broadcasted_iota(jnp.int32, (CHUNK, CHUNK), 1))
    w_heads = [jnp.where(tri, ws_ref[h], 0.0).astype(BF16) for h in range(ws_ref.shape[0])]
    for s in range(tm // tk):
        r0 = s * tk
        h = _rms(x_ref[r0:r0 + tk, :], g_ref[...]).astype(BF16)
        pa = _dot(h, wa_ref[...]) + ba_ref[...]
        q_ref[r0:r0 + tk, :] = (pa[:, aw2:] * q_scale).astype(BF16)
        kvt = _dot_nt(wkv_ref[...], h) + bkv_ref[...]
        kt_ref[:, r0:r0 + tk] = kvt[:bw]
        vt_ref[:, r0:r0 + tk] = kvt[bw:]
        kvtb_ref[s] = kvt.astype(BF16)
        c3 = _dot(h, wc_ref[...]) + bc_ref[...]
        _mixer_prompt_rows(pa[:, :aw2], c3, r0, gv, w_heads, bs_ref, cw_ref, ya_ref, yc_ref, g_scr)
    cnew_ref[...] = g_scr[tm + 6:tm + 8, :]


def _in_proj_prompt(x, g, wa, ba, wkvt, bkvt, wc, bc, gv, ws, bs_full, cw, cbuf, kt_all, vt_all, layer,
                    *, nb, tm, tk, aw2):
    m, d = x.shape
    t = m // nb
    n_i = t // tm
    na = wa.shape[1]
    bw = na - aw2
    aw = aw2 // 2
    nc = wc.shape[1]
    cww = nc // 3
    const = lambda b, i: (0, 0)
    row = lambda b, i: (b * n_i + i, 0)
    seq = lambda b, i: (b, 0, 0)
    return pl.pallas_call(
        functools.partial(_in_proj_prompt_kernel, aw2=aw2, bw=bw, q_scale=HEAD_DIM ** -0.5 * LOG2E, tk=tk),
        grid=(nb, n_i),
        in_specs=[
            pl.BlockSpec((tm, d), row),
            pl.BlockSpec((1, d), const),
            pl.BlockSpec((d, na), const), pl.BlockSpec((1, na), const),
            pl.BlockSpec(wkvt.shape, const), pl.BlockSpec(bkvt.shape, const),
            pl.BlockSpec((d, nc), const), pl.BlockSpec((1, nc), const),
            pl.BlockSpec((1, aw), const),
            pl.BlockSpec(ws.shape, lambda b, i: (0, 0, 0)),
            pl.BlockSpec(bs_full.shape, const),
            pl.BlockSpec(cw.shape, const),
            pl.BlockSpec((None, CONV_K - 1, cww), seq),
            pl.BlockSpec(memory_space=pl.ANY), pl.BlockSpec(memory_space=pl.ANY),
        ],
        out_specs=[
            pl.BlockSpec((tm, bw), row),
            pl.BlockSpec((None, None, bw, tm), lambda b, i: (layer, b, 0, i)),
            pl.BlockSpec((None, None, bw, tm), lambda b, i: (layer, b, 0, i)),
            pl.BlockSpec((None, tm // tk, 2 * bw, tk), lambda b, i: (b, i, 0, 0)),
            pl.BlockSpec((tm, aw), row),
            pl.BlockSpec((tm, cww), row),
            pl.BlockSpec((None, CONV_K - 1, cww), seq),
        ],
        out_shape=[
            jax.ShapeDtypeStruct((m, bw), BF16),
            jax.ShapeDtypeStruct(kt_all.shape, F32), jax.ShapeDtypeStruct(vt_all.shape, F32),
            jax.ShapeDtypeStruct((nb, t // tk, 2 * bw, tk), BF16),
            jax.ShapeDtypeStruct((m, aw), BF16),
            jax.ShapeDtypeStruct((m, cww), BF16),
            jax.ShapeDtypeStruct((nb, CONV_K - 1, cww), F32),
        ],
        input_output_aliases={13: 1, 14: 2},
        scratch_shapes=[pltpu.VMEM((tm + 8, cww), F32)],
        compiler_params=_cparams(("parallel", "arbitrary")),
        name="in_proj_prompt",
    )(x, g, wa, ba, wkvt, bkvt, wc, bc, gv, ws, bs_full, cw, cbuf, kt_all, vt_all)


def _mixer_sample_kernel(auv_ref, c3_ref, gv_ref, wv_ref, bv_ref, cw_ref, cbuf_ref,
                         ya_ref, yc_ref, cnew_ref, chv_ref, *, aw, cw_width, n_t, n_seq):
    u, vn = _gelu_u_vn(auv_ref[...], gv_ref[...], aw)
    chv_ref[...] = vn
    c3 = c3_ref[...]
    gated = c3[:, cw_width:2 * cw_width] * c3[:, 2 * cw_width:]
    blk = lambda a, t: a[t * n_seq:(t + 1) * n_seq, :]
    gp = [cbuf_ref[0], cbuf_ref[1]] + [blk(gated, t) for t in range(n_t)]
    for t in range(n_t):
        mixed = bv_ref[t:t + 1, :]
        for s in range(t + 1):
            mixed = mixed + wv_ref[t * n_t + s:t * n_t + s + 1, :] * blk(vn, s)
        rows = pl.ds(t * n_seq, n_seq)
        ya_ref[rows, :] = (blk(u, t) * mixed).astype(ya_ref.dtype)
        conv = cw_ref[0:1, :] * gp[t] + cw_ref[1:2, :] * gp[t + 1] + cw_ref[2:3, :] * gp[t + 2]
        yc_ref[rows, :] = (blk(c3, t)[:, :cw_width] * conv).astype(yc_ref.dtype)
    cnew_ref[0] = gp[n_t]
    cnew_ref[1] = gp[n_t + 1]


def _mixer_sample(auv, c3, gv, wvec, bvec, cw, cbuf_t, *, n_t, n_seq):
    m = auv.shape[0]
    aw = auv.shape[1] // 2
    cww = c3.shape[1] // 3
    return pl.pallas_call(
        functools.partial(_mixer_sample_kernel, aw=aw, cw_width=cww, n_t=n_t, n_seq=n_seq),
        out_shape=[jax.ShapeDtypeStruct((m, aw), BF16),
                   jax.ShapeDtypeStruct((m, cww), BF16),
                   jax.ShapeDtypeStruct((CONV_K - 1, n_seq, cww), F32),
                   jax.ShapeDtypeStruct((m, aw), F32)],
        compiler_params=pltpu.CompilerParams(vmem_limit_bytes=VMEM_LIMIT),
        name="mixer_sample",
    )(auv, c3, gv, wvec, bvec, cw, cbuf_t)


def _sb_block(z2, mask, u_mat, carry, terms):
    lb, lr = _log2_sig_pair(z2)
    if mask is not None:
        lr = jnp.where(mask, lr, 0.0)
    between = _split_dot(lr, u_mat, terms) + carry
    a = jnp.exp2(lb + between)
    if mask is not None:
        a = jnp.where(mask, a, 0.0)
    return a, carry + jnp.sum(lr, axis=-1, keepdims=True)


def _sb_prompt_kernel(q_ref, kt_ref, vt_ref, o_ref, carry_ref, acc_ref, *, tq, tk, terms):
    qi = pl.program_id(2)
    u_mat = _strict_upper(tk)
    q = q_ref[...]
    heads = q.shape[1] // HEAD_DIM
    qh = [q[:, h * HEAD_DIM:(h + 1) * HEAD_DIM] for h in range(heads)]
    carry_ref[...] = jnp.zeros_like(carry_ref)
    acc_ref[...] = jnp.zeros_like(acc_ref)

    def step(j, r0, nr, triangular):
        rows = pl.ds(r0, nr)
        mask = None
        if triangular:
            mask = (lax.broadcasted_iota(jnp.int32, (nr, tk), 1)
                    < lax.broadcasted_iota(jnp.int32, (nr, tk), 0))
        for h in range(heads):
            hs = pl.ds(h * HEAD_DIM, HEAD_DIM)
            z = _dot(qh[h][r0:r0 + nr], kt_ref[j, hs, :])
            a, carry = _sb_block(z, mask, u_mat, carry_ref[h, rows], terms)
            carry_ref[h, rows] = carry
            acc_ref[h, rows] += _dot_nt(a.astype(BF16), vt_ref[j, hs, :])

    n_diag = tq // tk
    first_diag = qi * n_diag
    for r in range(n_diag):
        step(first_diag + r, r * tk, tk, True)
        for d in range(r - 1, -1, -1):
            step(first_diag + d, r * tk, tk, False)

    def body(n, c):
        for d in range(n_diag):
            step(first_diag - 1 - n * n_diag - d, 0, tq, False)
        return c

    lax.fori_loop(0, qi, body, 0)
    o_ref[...] = jnp.concatenate([acc_ref[h] for h in range(heads)], axis=1).astype(o_ref.dtype)


def _sb_prompt(q, kvtb, *, nb, bw, terms):
    m = q.shape[0]
    t = m // nb
    n_kb, tk = kvtb.shape[1], kvtb.shape[3]
    tq = SB_TQ
    n_q = t // tq
    gw = SB_HEADS_PER_STEP * HEAD_DIM
    n_hg = bw // gw
    return pl.pallas_call(
        functools.partial(_sb_prompt_kernel, tq=tq, tk=tk, terms=terms),
        grid=(nb, n_hg, n_q),
        in_specs=[pl.BlockSpec((tq, gw), lambda b, g, i: (b * n_q + i, g)),
                  pl.BlockSpec((None, n_kb, gw, tk), lambda b, g, i: (b, 0, g, 0)),
                  pl.BlockSpec((None, n_kb, gw, tk), lambda b, g, i: (b, 0, n_hg + g, 0))],
        out_specs=pl.BlockSpec((tq, gw), lambda b, g, i: (b * n_q + i, g)),
        out_shape=jax.ShapeDtypeStruct((m, bw), BF16),
        scratch_shapes=[pltpu.VMEM((SB_HEADS_PER_STEP, tq, 1), F32),
                        pltpu.VMEM((SB_HEADS_PER_STEP, tq, HEAD_DIM), F32)],
        compiler_params=_cparams(("parallel", "parallel", "parallel")),
        name="sb_prompt",
    )(q, kvtb, kvtb)


def _sb_sample_kernel(pt_ref, qbd_ref, kvn_ref, *rest, n_pages, n_heads, bw):
    k_refs = rest[:n_pages]
    v_refs = rest[n_pages:2 * n_pages]
    o_ref = rest[2 * n_pages]
    qbd = qbd_ref[...]
    rows = qbd.shape[0]
    page = k_refs[0].shape[1]
    n_slots = n_pages + 1
    u_mat = _strict_upper(page)

    kvn = kvn_ref[...]
    pad = jnp.zeros((page - kvn.shape[0], bw), F32)
    kn = jnp.concatenate([kvn[:, :bw], pad], axis=0).astype(BF16)
    vn = jnp.concatenate([kvn[:, bw:], pad], axis=0).astype(BF16)
    t_idx = lax.broadcasted_iota(jnp.int32, (rows, page), 0) // n_heads
    new_mask = lax.broadcasted_iota(jnp.int32, (rows, page), 1) < t_idx

    z = [_dot_nt(qbd, kn)] + [_dot(qbd, k_refs[i][...].astype(BF16)) for i in range(n_pages)]
    lb, lr = _log2_sig_pair(jnp.concatenate(z, axis=1))
    lr = [lr[:, s * page:(s + 1) * page] for s in range(n_slots)]
    lr[0] = jnp.where(new_mask, lr[0], 0.0)
    between = _dot(jnp.concatenate(lr, axis=0).astype(BF16), u_mat)
    carry = jnp.zeros((rows, 1), F32)
    x = []
    for s in range(n_slots):
        x.append(lb[:, s * page:(s + 1) * page] + between[s * rows:(s + 1) * rows] + carry)
        carry = carry + jnp.sum(lr[s], axis=-1, keepdims=True)
    a = jnp.exp2(jnp.concatenate(x, axis=1))
    a = [a[:, s * page:(s + 1) * page] for s in range(n_slots)]
    acc = _dot(jnp.where(new_mask, a[0], 0.0).astype(BF16), vn)
    for i in range(n_pages):
        acc = acc + _dot_nt(a[i + 1].astype(BF16), v_refs[i][...].astype(BF16))

    r_head = lax.broadcasted_iota(jnp.int32, acc.shape, 0) % n_heads
    l_head = lax.broadcasted_iota(jnp.int32, acc.shape, 1) // HEAD_DIM
    kept = jnp.where(r_head == l_head, acc, 0.0)
    o_ref[...] = jnp.sum(kept.reshape(rows // n_heads, n_heads, bw), axis=1)


def _sb_sample(page_table_flat, qbd, kv_new, cache_kt, cache_vt, layer, *, n_pages, n_heads):
    n_seq, rows, bw = qbd.shape
    page = cache_kt.shape[3]
    n_t = rows // n_heads

    def page_spec(i):
        def idx(b, pt):
            return (layer, pt[b * n_pages + (n_pages - 1 - i)], 0, 0)
        return pl.BlockSpec((None, None, bw, page), idx)

    seq3 = lambda b, pt: (b, 0, 0)
    grid_spec = pltpu.PrefetchScalarGridSpec(
        num_scalar_prefetch=1,
        grid=(n_seq,),
        in_specs=[pl.BlockSpec((None, rows, bw), seq3),
                  pl.BlockSpec((None,) + kv_new.shape[1:], seq3)]
                 + [page_spec(i) for i in range(n_pages)] + [page_spec(i) for i in range(n_pages)],
        out_specs=pl.BlockSpec((None, n_t, bw), seq3),
    )
    return pl.pallas_call(
        functools.partial(_sb_sample_kernel, n_pages=n_pages, n_heads=n_heads, bw=bw),
        grid_spec=grid_spec,
        out_shape=jax.ShapeDtypeStruct((n_seq, n_t, bw), F32),
        compiler_params=_cparams(("parallel",)),
        name="sb_sample",
    )(page_table_flat, qbd, kv_new, *([cache_kt] * n_pages), *([cache_vt] * n_pages))


def _mix_xattn_prompt_kernel(x_ref, ya_ref, yb_ref, yc_ref, woa_ref, wob_ref, woc_ref, g_ref, wq_ref,
                             k_ref, v_ref, wxo_ref, o_ref, att_ref, *, n_heads, hd, q_scale):
    x1 = (x_ref[...] + _dot(ya_ref[...], woa_ref[...]) + _dot(yb_ref[...], wob_ref[...])
          + _dot(yc_ref[...], woc_ref[...]))
    hq = (_dot(_rms(x1, g_ref[...]).astype(BF16), wq_ref[...]) * q_scale).astype(BF16)
    for h in range(n_heads):
        cols = pl.ds(h * hd, hd)
        s = _dot_nt(hq[:, h * hd:(h + 1) * hd], k_ref[:, cols])
        e = jnp.exp(s - jnp.max(s, axis=-1, keepdims=True))
        p = e / jnp.sum(e, axis=-1, keepdims=True)
        att_ref[:, cols] = _dot(p.astype(BF16), v_ref[:, cols]).astype(BF16)
    o_ref[...] = x1 + _dot(att_ref[...], wxo_ref[...])


def _mix_xattn_prompt(x, ys, wos, g, wq, kv, wxo, *, nb, n_mem, n_heads, tm):
    m, d = x.shape
    n_i = (m // nb) // tm
    row = lambda b, i: (b * n_i + i, 0)
    const = lambda b, i: (0, 0)
    return pl.pallas_call(
        functools.partial(_mix_xattn_prompt_kernel, n_heads=n_heads, hd=d // n_heads,
                          q_scale=(d // n_heads) ** -0.5),
        grid=(nb, n_i),
        in_specs=[pl.BlockSpec((tm, d), row)]
                 + [pl.BlockSpec((tm, y.shape[1]), row) for y in ys]
                 + [pl.BlockSpec(w.shape, const) for w in wos]
                 + [pl.BlockSpec((1, d), const), pl.BlockSpec((d, d), const),
                    pl.BlockSpec((n_mem, d), lambda b, i: (b, 0)),
                    pl.BlockSpec((n_mem, d), lambda b, i: (b, 1)),
                    pl.BlockSpec((d, d), const)],
        out_specs=pl.BlockSpec((tm, d), row),
        out_shape=jax.ShapeDtypeStruct((m, d), F32),
        scratch_shapes=[pltpu.VMEM((tm, d), BF16)],
        compiler_params=_cparams(("parallel", "parallel")),
        name="mix_xattn_prompt",
    )(x, *ys, *wos, g, wq, kv, kv, wxo)


def _xattn_sample_kernel(q_ref, k_ref, v_ref, o_ref, *, n_heads):
    n_sq, n_mem, _, hd = k_ref.shape
    for i in range(n_sq):
        k = k_ref[i].reshape(n_mem * n_heads, hd).astype(BF16)
        v = v_ref[i].reshape(n_mem * n_heads, hd).astype(BF16)
        s = _dot_nt(q_ref[i], k)
        r_head = lax.broadcasted_iota(jnp.int32, s.shape, 0) % n_heads
        c_head = lax.broadcasted_iota(jnp.int32, s.shape, 1) % n_heads
        s = jnp.where(r_head == c_head, s, -jnp.inf)
        e = jnp.exp(s - jnp.max(s, axis=-1, keepdims=True))
        p = e / jnp.sum(e, axis=-1, keepdims=True)
        o_ref[i] = _dot(p.astype(BF16), v).astype(o_ref.dtype)


def _xattn_sample(q, k, v, layer, *, n_heads):
    n_seq, rows, hd = q.shape
    n_mem = k.shape[2]
    n_sq = XATTN_SEQS_PER_STEP if n_seq % XATTN_SEQS_PER_STEP == 0 else 1
    kv_spec = pl.BlockSpec((None, n_sq, n_mem, n_heads, hd), lambda b: (layer, b, 0, 0, 0))
    return pl.pallas_call(
        functools.partial(_xattn_sample_kernel, n_heads=n_heads),
        grid=(n_seq // n_sq,),
        in_specs=[pl.BlockSpec((n_sq, rows, hd), lambda b: (b, 0, 0)), kv_spec, kv_spec],
        out_specs=pl.BlockSpec((n_sq, rows, hd), lambda b: (b, 0, 0)),
        out_shape=jax.ShapeDtypeStruct((n_seq, rows, hd), BF16),
        compiler_params=_cparams(("parallel",)),
        name="xattn_sample",
    )(q, k, v)


def _row_tile(m, cap):
    t = min(m, cap)
    while m % t:
        t //= 2
    return t


def kernel(x_prompt, x_sample, cache_sb_k, cache_sb_v, cache_mem_k, cache_mem_v, state_conv, page_table,
           mem_prompt, g_mix, w_in, b_in, g_vnorm, w_spatial, b_spatial, conv_w, w_out, g_xattn, g_mem,
           w_xq, w_xkv, w_xo, g_ffn, w_up, w_down, g_final):
    nb, seq, d = x_prompt.shape
    n_seq, n_t, _ = x_sample.shape
    depth, n_pool, page, b_heads, hd = cache_sb_k.shape
    assert hd == HEAD_DIM and page == LANES
    n_pages = page_table.shape[1]
    a_heads = g_vnorm.shape[1]
    aw = a_heads * HEAD_DIM
    bw = b_heads * HEAD_DIM
    cww = conv_w.shape[2]
    n_mem, mem_heads, mem_hd = cache_mem_k.shape[2:]
    sb_terms = 1

    mp = nb * seq
    ms = n_seq * n_t
    xp = x_prompt.reshape(mp, d)
    xs = x_sample.transpose(1, 0, 2).reshape(ms, d)
    mem = mem_prompt.reshape(nb * n_mem, d)

    cache_kt = cache_sb_k.transpose(0, 1, 3, 4, 2).reshape(depth, n_pool, bw, page)
    cache_vt = cache_sb_v.transpose(0, 1, 3, 4, 2).reshape(depth, n_pool, bw, page)
    pt_flat = page_table.reshape(-1)
    head_eye = jnp.eye(b_heads, dtype=BF16)
    wu_all, wd_all = w_up.astype(BF16), w_down.astype(BF16)

    row2 = lambda v: v.reshape(1, -1)
    tm_p = _row_tile(seq, 1024)
    tm_s = _row_tile(ms, 512)
    ka, kb = 2 * aw + bw, 2 * aw + 3 * bw

    outs = {k: [] for k in ("ks", "vs", "cp", "cs", "chv", "mk", "mv")}
    conv_zero = jnp.zeros((nb, CONV_K - 1, cww), F32)
    kt_all = jnp.zeros((depth, nb, bw, seq), F32)
    vt_all = jnp.zeros((depth, nb, bw, seq), F32)

    for l in range(depth):
        last = l == depth - 1
        wa = w_in[l][:, :ka].astype(BF16)
        wkv = w_in[l][:, ka:kb].astype(BF16)
        wkvt = wkv.T
        wc = w_in[l][:, kb:].astype(BF16)
        ba, bkv, bc = row2(b_in[l][:ka]), row2(b_in[l][ka:kb]), row2(b_in[l][kb:])
        bkvt = bkv.reshape(-1, 1)
        wo = w_out[l].astype(BF16)
        wo_parts = [wo[:aw], wo[aw:aw + bw], wo[aw + bw:]]
        wxq, wxo = w_xq[l].astype(BF16), w_xo[l].astype(BF16)
        gv = g_vnorm[l].reshape(1, aw)
        bs_full = jnp.repeat(b_spatial[l].T, HEAD_DIM, axis=1)
        wvec = jnp.repeat(w_spatial[l][:, :n_t, :n_t].transpose(1, 2, 0), HEAD_DIM, axis=2).reshape(n_t * n_t, aw)
        bvec = jnp.repeat(b_spatial[l][:, :n_t].T, HEAD_DIM, axis=1)
        xq_scale = mem_hd ** -0.5

        kv_f32, kv_b = _norm_matmul(mem, row2(g_mem[l]), w_xkv[l].astype(BF16),
                                    tm=_row_tile(nb * n_mem, 512), out_dtypes=(F32, BF16))
        q, kt_all, vt_all, kvtb, ya, yc, cnew = _in_proj_prompt(
            xp, row2(g_mix[l]), wa, ba, wkvt, bkvt, wc, bc, gv, w_spatial[l], bs_full, conv_w[l], conv_zero,
            kt_all, vt_all, l, nb=nb, tm=tm_p, tk=SB_TK, aw2=2 * aw)
        yb = _sb_prompt(q, kvtb, nb=nb, bw=bw, terms=sb_terms)
        xp = _mix_xattn_prompt(xp, (ya, yb, yc), wo_parts, row2(g_xattn[l]), wxq, kv_b, wxo,
                               nb=nb, n_mem=n_mem, n_heads=mem_heads, tm=_row_tile(seq, 1024))
        xp = _ffn(xp, row2(g_ffn[l]), wu_all, wd_all, row2(g_final), l,
                  tm=_row_tile(mp, 512), tf=2048, final_norm=last)
        outs["cp"].append(cnew)
        outs["mk"].append(kv_f32[:, :d].reshape(nb, n_mem, mem_heads, mem_hd))
        outs["mv"].append(kv_f32[:, d:].reshape(nb, n_mem, mem_heads, mem_hd))

        auv, q, c3, kv = _in_proj_sample(xs, row2(g_mix[l]), wa, ba, wkv, bkv, wc, bc, tm=tm_s, aw2=2 * aw)
        ya, yc, cnew, chv = _mixer_sample(auv, c3, gv, wvec, bvec, conv_w[l],
                                          state_conv[l].transpose(1, 0, 2), n_t=n_t, n_seq=n_seq)
        q_b = q.reshape(n_t, n_seq, b_heads, HEAD_DIM).transpose(1, 0, 2, 3)
        qbd = (q_b[:, :, :, None, :] * head_eye[None, None, :, :, None]).reshape(n_seq, n_t * b_heads, bw)
        kv_b = kv.reshape(n_t, n_seq, 2 * bw).transpose(1, 0, 2)
        kv_pad = jnp.pad(kv_b, ((0, 0), (0, 8 - n_t), (0, 0)))
        yb_b = _sb_sample(pt_flat, qbd, kv_pad, cache_kt, cache_vt, l,
                          n_pages=n_pages, n_heads=b_heads)
        yb = yb_b.transpose(1, 0, 2).reshape(ms, bw).astype(BF16)
        xs = _matmul_res(xs, [ya, yb, yc], wo_parts, tm=tm_s)
        hq, = _norm_matmul(xs, row2(g_xattn[l]), wxq, tm=tm_s, out_dtypes=(BF16,), scale=xq_scale)
        hq_b = hq.reshape(n_t, n_seq, mem_heads, mem_hd).transpose(1, 0, 2, 3).reshape(n_seq, n_t * mem_heads, mem_hd)
        o_b = _xattn_sample(hq_b, cache_mem_k, cache_mem_v, l, n_heads=mem_heads)
        o = o_b.reshape(n_seq, n_t, d).transpose(1, 0, 2).reshape(ms, d)
        xs = _matmul_res(xs, [o], [wxo], tm=tm_s)
        xs = _ffn(xs, row2(g_ffn[l]), wu_all, wd_all, row2(g_final), l, tm=tm_s, tf=1024, final_norm=last)
        outs["ks"].append(kv_b[:, :, :bw].reshape(n_seq, n_t, b_heads, HEAD_DIM))
        outs["vs"].append(kv_b[:, :, bw:].reshape(n_seq, n_t, b_heads, HEAD_DIM))
        outs["cs"].append(cnew.transpose(1, 0, 2))
        outs["chv"].append(chv.reshape(n_t, n_seq, a_heads, HEAD_DIM).transpose(1, 0, 2, 3))

    y_prompt = xp.reshape(nb, seq, d)
    y_sample = xs.reshape(n_t, n_seq, d).transpose(1, 0, 2)
    st = lambda k: jnp.stack(outs[k])
    kp = kt_all.reshape(depth, nb, b_heads, HEAD_DIM, seq).transpose(0, 1, 4, 2, 3)
    vp = vt_all.reshape(depth, nb, b_heads, HEAD_DIM, seq).transpose(0, 1, 4, 2, 3)
    return (y_prompt, y_sample, kp, vp, st("ks"), st("vs"), st("cp"), st("cs"), st("chv"),
            st("mk"), st("mv"))
```
